```python
import jax, jax.numpy as jnp
from jax import lax
import numpy as np

D_MODEL = 1024
BATCH = 4
SEQ = 8192
DEPTH = 4

MEM_LEN = 256
GLA_HEADS = 4
GLA_DK = 64
GLA_DV = 128
GLA_RANK = 16
GLA_TAU = 16.0
GLA_CHUNK = 64
GLA_QK = GLA_HEADS * GLA_DK
GLA_V = GLA_HEADS * GLA_DV
FOX_HEADS = 4
FOX_HD = 128
FOX_W = FOX_HEADS * FOX_HD
FOX_BLOCK = 128
MEM_HEADS = 4
MEM_HD = 128
MEM_W = MEM_HEADS * MEM_HD
N_BRANCH = 3
BRANCH_W = 512
D_FF = -(-8 * D_MODEL // (3 * 256)) * 256
EPS = 1e-6

IN_SIZES = (GLA_QK, GLA_QK, GLA_V, GLA_V, GLA_RANK,
            FOX_W, FOX_W, FOX_W, FOX_HEADS,
            MEM_W,
            N_BRANCH * D_MODEL)
IN_WIDTH = sum(IN_SIZES)
IN_OFFSETS = tuple(int(o) for o in np.cumsum(IN_SIZES)[:-1])

kernel_name = "hybrid_gla_fox_mem_parallel_trunk"


def _rmsnorm(x, g):
    xf = x.astype(jnp.float32)
    y = xf * lax.rsqrt(jnp.mean(xf * xf, axis=-1, keepdims=True) + EPS)
    return (y * g.astype(jnp.float32)).astype(x.dtype)


def _heads(t, n):
    b, s, w = t.shape
    return t.reshape(b, s, n, w // n).transpose(0, 2, 1, 3)


def _merge(t):
    b, h, s, d = t.shape
    return t.transpose(0, 2, 1, 3).reshape(b, s, h * d)


def _gla(q, k, v, log_a):
    out_dtype = v.dtype
    b, h, s, dk = q.shape
    dv = v.shape[-1]
    c = GLA_CHUNK
    n = s // c
    qc = (q.astype(jnp.float32) * (GLA_DK ** -0.5)).reshape(b, h, n, c, dk)
    kc = k.astype(jnp.float32).reshape(b, h, n, c, dk)
    vc = v.astype(jnp.float32).reshape(b, h, n, c, dv)
    cum = jnp.cumsum(log_a.astype(jnp.float32).reshape(b, h, n, c, dk), axis=-2)
    cum_last = cum[..., -1:, :]
    q_in = qc * jnp.exp(cum)
    k_in = kc * jnp.exp(-cum)
    k_out = kc * jnp.exp(cum_last - cum)
    tril = jnp.tril(jnp.ones((c, c), dtype=bool))
    attn = jnp.where(tril, jnp.einsum('bhncd,bhnsd->bhncs', q_in, k_in), 0.0)
    o_intra = jnp.einsum('bhncs,bhnsv->bhncv', attn, vc)
    chunk_state = jnp.einsum('bhncd,bhncv->bhndv', k_out, vc)
    decay = jnp.exp(cum_last[..., 0, :])

    def step(state, inp):
        d, cs = inp
        return d[..., None] * state + cs, state

    _, s_in = lax.scan(step, jnp.zeros((b, h, dk, dv), jnp.float32),
                       (jnp.moveaxis(decay, 2, 0), jnp.moveaxis(chunk_state, 2, 0)))
    s_in = jnp.moveaxis(s_in, 0, 2)
    o_inter = jnp.einsum('bhncd,bhndv->bhncv', q_in, s_in)
    return (o_intra + o_inter).reshape(b, h, s, dv).astype(out_dtype)


def _fox(q, k, v, log_f):
    b, h, s, hd = q.shape
    nb = s // FOX_BLOCK
    scale = hd ** -0.5
    cum = jnp.cumsum(log_f, axis=-1)
    qb = jnp.moveaxis(q.reshape(b, h, nb, FOX_BLOCK, hd), 2, 0)
    cb = jnp.moveaxis(cum.reshape(b, h, nb, FOX_BLOCK), 2, 0)
    starts = jnp.arange(nb, dtype=jnp.int32) * FOX_BLOCK
    kpos = jnp.arange(s, dtype=jnp.int32)

    def block(args):
        q_i, c_i, i0 = args
        logits = (jnp.einsum('bhqd,bhkd->bhqk', q_i, k).astype(jnp.float32) * scale
                  + c_i[..., None] - cum[:, :, None, :])
        qpos = i0 + jnp.arange(FOX_BLOCK, dtype=jnp.int32)
        mask = kpos[None, :] <= qpos[:, None]
        p = jax.nn.softmax(jnp.where(mask, logits, -jnp.inf), axis=-1)
        return jnp.einsum('bhqk,bhkd->bhqd', p.astype(v.dtype), v)

    out = lax.map(block, (qb, cb, starts))
    return jnp.moveaxis(out, 0, 2).reshape(b, h, s, hd)


def _cross(q, k, v):
    logits = jnp.einsum('bhqd,bhkd->bhqk', q, k).astype(jnp.float32) * (q.shape[-1] ** -0.5)
    p = jax.nn.softmax(logits, axis=-1)
    return jnp.einsum('bhqk,bhkd->bhqd', p.astype(v.dtype), v)


def setup_inputs(seed: int = 0) -> dict:
    key = jax.random.key(seed)
    ks = jax.random.split(key, 24)
    L, D = DEPTH, D_MODEL

    def nrm(k, shape, scale):
        return jax.random.normal(k, shape, jnp.float32) * scale

    def gain(k, shape):
        return 1.0 + 0.02 * jax.random.normal(k, shape, jnp.float32)

    return {
        "x": nrm(ks[0], (BATCH, SEQ, D), 1.0),
        "mem": nrm(ks[1], (BATCH, MEM_LEN, D), 1.0),
        "g_mix": gain(ks[2], (L, D)),
        "w_in": nrm(ks[3], (L, D, IN_WIDTH), D ** -0.5),
        "w_gla_a2": nrm(ks[4], (L, GLA_RANK, GLA_QK), GLA_RANK ** -0.5),
        "b_gla_a": nrm(ks[5], (L, GLA_QK), 0.1),
        "g_gla_out": gain(ks[6], (L, GLA_V)),
        "b_fox_f": nrm(ks[7], (L, FOX_HEADS), 0.1),
        "g_fox_q": gain(ks[8], (L, FOX_HD)),
        "g_fox_k": gain(ks[9], (L, FOX_HD)),
        "g_mem": gain(ks[10], (L, D)),
        "w_mem_kv": nrm(ks[11], (L, D, 2 * MEM_W), D ** -0.5),
        "g_mem_q": gain(ks[12], (L, MEM_HD)),
        "g_mem_k": gain(ks[13], (L, MEM_HD)),
        "b_gate": nrm(ks[14], (L, N_BRANCH * D), 0.1),
        "w_branch": nrm(ks[15], (L, N_BRANCH, BRANCH_W, D), BRANCH_W ** -0.5),
        "w_out": nrm(ks[16], (L, D, D), D ** -0.5),
        "g_ffn": gain(ks[17], (L, D)),
        "w_ffn_gate": nrm(ks[18], (L, D, D_FF), D ** -0.5),
        "w_ffn_up": nrm(ks[19], (L, D, D_FF), D ** -0.5),
        "w_ffn_down": nrm(ks[20], (L, D_FF, D), D_FF ** -0.5),
    }


def reference(x, mem, g_mix, w_in, w_gla_a2, b_gla_a, g_gla_out, b_fox_f, g_fox_q, g_fox_k,
              g_mem, w_mem_kv, g_mem_q, g_mem_k, b_gate, w_branch, w_out,
              g_ffn, w_ffn_gate, w_ffn_up, w_ffn_down):
    for l in range(DEPTH):
        h = _rmsnorm(x, g_mix[l])
        proj = h @ w_in[l]
        (gq, gk, gv, gg, ga1, fq, fk, fv, ff, mq, bg) = jnp.split(proj, IN_OFFSETS, axis=-1)

        log_a = jax.nn.log_sigmoid((ga1 @ w_gla_a2[l] + b_gla_a[l]).astype(jnp.float32)) / GLA_TAU
        o_gla = _gla(_heads(gq, GLA_HEADS), _heads(gk, GLA_HEADS), _heads(gv, GLA_HEADS),
                     _heads(log_a, GLA_HEADS))
        o_gla = _rmsnorm(o_gla, g_gla_out[l].reshape(GLA_HEADS, 1, GLA_DV))
        y_gla = _merge(o_gla) * jax.nn.silu(gg)

        log_f = jax.nn.log_sigmoid((ff + b_fox_f[l]).astype(jnp.float32)).transpose(0, 2, 1)
        o_fox = _fox(_rmsnorm(_heads(fq, FOX_HEADS), g_fox_q[l]),
                     _rmsnorm(_heads(fk, FOX_HEADS), g_fox_k[l]),
                     _heads(fv, FOX_HEADS), log_f)
        y_fox = _merge(o_fox)

        mkv = _rmsnorm(mem, g_mem[l]) @ w_mem_kv[l]
        mk, mv = jnp.split(mkv, 2, axis=-1)
        o_mem = _cross(_rmsnorm(_heads(mq, MEM_HEADS), g_mem_q[l]),
                       _rmsnorm(_heads(mk, MEM_HEADS), g_mem_k[l]),
                       _heads(mv, MEM_HEADS))
        y_mem = _merge(o_mem)

        gates = jax.nn.sigmoid(bg + b_gate[l])
        merged = None
        for i, y_b in enumerate((y_gla, y_fox, y_mem)):
            term = gates[..., i * D_MODEL:(i + 1) * D_MODEL] * (y_b @ w_branch[l, i])
            merged = term if merged is None else merged + term
        x = x + merged @ w_out[l]

        h2 = _rmsnorm(x, g_ffn[l])
        x = x + (jax.nn.silu(h2 @ w_ffn_gate[l]) * (h2 @ w_ffn_up[l])) @ w_ffn_down[l]
    return x
```

```python
import functools

import jax
import jax.numpy as jnp
from jax import lax
from jax.experimental import pallas as pl
from jax.experimental.pallas import tpu as pltpu

EPS = 1e-6
GLA_HEADS = 4
GLA_DK = 64
GLA_DV = 128
GLA_RANK = 16
GLA_TAU = 16.0
GLA_CHUNK = 64
GLA_QK = GLA_HEADS * GLA_DK
GLA_V = GLA_HEADS * GLA_DV
FOX_HEADS = 4
FOX_HD = 128
FOX_W = FOX_HEADS * FOX_HD
MEM_HEADS = 4
MEM_HD = 128
MEM_W = MEM_HEADS * MEM_HD
N_BRANCH = 3

LANES = 128
SUBLANES = 8
VMEM_LIMIT_BYTES = 60000 * 1024

SMALL_W = LANES
SMALL_FF = 0
SMALL_GA = SUBLANES
GLA_IN_W = 3 * GLA_QK + 2 * GLA_V

BF16 = jnp.bfloat16
F32 = jnp.float32


def _dot(a, b):
    return jnp.dot(a, b, preferred_element_type=F32)


def _dot_nt(a, b):
    return lax.dot_general(a, b, (((1,), (1,)), ((), ())), preferred_element_type=F32)


def _dot_tn(a, b):
    return lax.dot_general(a, b, (((0,), (0,)), ((), ())), preferred_element_type=F32)


def _rms(x):
    return x * lax.rsqrt(jnp.mean(x * x, axis=-1, keepdims=True) + EPS)


def _log_sigmoid(x):
    return jnp.minimum(x, 0.0) - jnp.log1p(jnp.exp(-jnp.abs(x)))


def _sigmoid(x):
    return 1.0 / (1.0 + jnp.exp(-x))


def _head_rms(x, gain, heads, width):
    outs = []
    for h in range(heads):
        outs.append(_rms(x[:, h * width:(h + 1) * width]) * gain)
    return jnp.concatenate(outs, axis=-1)


def _params(sem):
    return pltpu.CompilerParams(dimension_semantics=sem, vmem_limit_bytes=VMEM_LIMIT_BYTES)


def _memkv_kernel(mem_ref, g_ref, w_ref, gk_ref, k_ref, v_ref):
    h = (_rms(mem_ref[0]) * g_ref[0]).astype(BF16)
    kv = _dot(h, w_ref[0])
    k_ref[0, 0] = _head_rms(kv[:, :MEM_W], gk_ref[0], MEM_HEADS, MEM_HD).astype(BF16)
    v_ref[0, 0] = kv[:, MEM_W:].astype(BF16)


def _memkv(mem, g_mem, w_kv, g_k):
    b, m, d = mem.shape
    depth = w_kv.shape[0]
    out = jax.ShapeDtypeStruct((depth, b, m, MEM_W), BF16)
    return pl.pallas_call(
        _memkv_kernel,
        grid=(depth, b),
        in_specs=[
            pl.BlockSpec((1, m, d), lambda l, i: (i, 0, 0)),
            pl.BlockSpec((1, 1, d), lambda l, i: (l, 0, 0)),
            pl.BlockSpec((1, d, 2 * MEM_W), lambda l, i: (l, 0, 0)),
            pl.BlockSpec((1, 1, MEM_HD), lambda l, i: (l, 0, 0)),
        ],
        out_specs=[
            pl.BlockSpec((1, 1, m, MEM_W), lambda l, i: (l, i, 0, 0)),
            pl.BlockSpec((1, 1, m, MEM_W), lambda l, i: (l, i, 0, 0)),
        ],
        out_shape=[out, out],
        compiler_params=_params(("arbitrary", "arbitrary")),
        name="memkv",
    )(mem, g_mem, w_kv, g_k)


def _lane_cumsum(x):
    lane = lax.broadcasted_iota(jnp.int32, x.shape, 1)
    shift = 1
    while shift < LANES:
        x = x + jnp.where(lane >= shift, pltpu.roll(x, shift, 1), 0.0)
        shift *= 2
    return x


def _proj_kernel(x_ref, g_ref, w_ref, w2_ref, ba_ref, bs_ref, gfq_ref, gfk_ref, gmq_ref,
                 gla_ref, fq_ref, fk_ref, fv_ref, mq_ref, cum_ref, carry_ref, *, tm):
    @pl.when(pl.program_id(1) == 0)
    def _():
        carry_ref[...] = jnp.zeros_like(carry_ref)

    h = (_rms(x_ref[0]) * g_ref[...]).astype(BF16)

    o_gla, o_fox, o_mq, o_small = 0, 2 * GLA_QK + 2 * GLA_V, 2 * GLA_QK + 2 * GLA_V + 3 * FOX_W, \
        2 * GLA_QK + 2 * GLA_V + 3 * FOX_W + MEM_W

    small = _dot(h, w_ref[:, o_small:o_small + SMALL_W])
    log_a = _log_sigmoid(_dot(small.astype(BF16), w2_ref[...]) + ba_ref[...]) / GLA_TAU

    p = _dot(h, w_ref[:, o_gla:o_gla + 2 * GLA_QK])
    gla_ref[0, :, 0:2 * GLA_QK] = p
    gla_ref[0, :, 2 * GLA_QK:3 * GLA_QK] = log_a
    gla_ref[0, :, 3 * GLA_QK:] = _dot(h, w_ref[:, o_gla + 2 * GLA_QK:o_fox])

    p = _dot(h, w_ref[:, o_fox:o_fox + FOX_W])
    fq_ref[0] = (_head_rms(p, gfq_ref[...], FOX_HEADS, FOX_HD) * (FOX_HD ** -0.5)).astype(BF16)
    p = _dot(h, w_ref[:, o_fox + FOX_W:o_fox + 2 * FOX_W])
    fk_ref[0] = _head_rms(p, gfk_ref[...], FOX_HEADS, FOX_HD).astype(BF16)
    fv_ref[0] = _dot(h, w_ref[:, o_fox + 2 * FOX_W:o_mq]).astype(BF16)
    p = _dot(h, w_ref[:, o_mq:o_small])
    mq_ref[0] = (_head_rms(p, gmq_ref[...], MEM_HEADS, MEM_HD) * (MEM_HD ** -0.5)).astype(BF16)

    log_f = _log_sigmoid(small + bs_ref[...])
    log_ft = log_f.T[0:SUBLANES, :]
    carry = carry_ref[...]
    blocks = []
    for j in range(tm // LANES):
        c = _lane_cumsum(log_ft[:, j * LANES:(j + 1) * LANES]) + carry
        blocks.append(c)
        carry = jnp.broadcast_to(c[:, LANES - 1:LANES], carry.shape)
    carry_ref[...] = carry
    cum = jnp.concatenate(blocks, axis=-1)
    for hd in range(FOX_HEADS):
        cum_ref[0, hd] = jnp.broadcast_to(cum[hd:hd + 1, :], (SUBLANES, tm))


def _proj(x, g_mix, w_a, w2p, b_a, b_small, g_fq, g_fk, g_mq, *, tm):
    b, s, d = x.shape
    wa = w_a.shape[1]
    const = lambda i, j: (0, 0)
    tok = lambda i, j: (i, j, 0)
    return pl.pallas_call(
        functools.partial(_proj_kernel, tm=tm),
        grid=(b, s // tm),
        in_specs=[
            pl.BlockSpec((1, tm, d), tok),
            pl.BlockSpec((1, d), const),
            pl.BlockSpec((d, wa), const),
            pl.BlockSpec((SMALL_W, GLA_QK), const),
            pl.BlockSpec((1, GLA_QK), const),
            pl.BlockSpec((1, SMALL_W), const),
            pl.BlockSpec((1, FOX_HD), const),
            pl.BlockSpec((1, FOX_HD), const),
            pl.BlockSpec((1, MEM_HD), const),
        ],
        out_specs=[
            pl.BlockSpec((1, tm, GLA_IN_W), tok),
            pl.BlockSpec((1, tm, FOX_W), tok),
            pl.BlockSpec((1, tm, FOX_W), tok),
            pl.BlockSpec((1, tm, FOX_W), tok),
            pl.BlockSpec((1, tm, MEM_W), tok),
            pl.BlockSpec((1, FOX_HEADS, SUBLANES, tm), lambda i, j: (i, 0, 0, j)),
        ],
        out_shape=[
            jax.ShapeDtypeStruct((b, s, GLA_IN_W), F32),
            jax.ShapeDtypeStruct((b, s, FOX_W), BF16),
            jax.ShapeDtypeStruct((b, s, FOX_W), BF16),
            jax.ShapeDtypeStruct((b, s, FOX_W), BF16),
            jax.ShapeDtypeStruct((b, s, MEM_W), BF16),
            jax.ShapeDtypeStruct((b, FOX_HEADS, SUBLANES, s), F32),
        ],
        scratch_shapes=[pltpu.VMEM((SUBLANES, LANES), F32)],
        compiler_params=_params(("arbitrary", "arbitrary")),
        name="proj",
    )(x, g_mix, w_a, w2p, b_a, b_small, g_fq, g_fk, g_mq)


def _split3(x):
    hi = x.astype(BF16)
    r = x - hi.astype(F32)
    mid = r.astype(BF16)
    lo = (r - mid.astype(F32)).astype(BF16)
    return hi, mid, lo


def _gla_kernel(in_ref, gain_ref, o_ref, st_ref, *, t2):
    @pl.when(pl.program_id(1) == 0)
    def _():
        st_ref[...] = jnp.zeros_like(st_ref)

    c = GLA_CHUNK
    hc = GLA_HEADS * c
    ri = lax.broadcasted_iota(jnp.int32, (c, c), 0)
    ci = lax.broadcasted_iota(jnp.int32, (c, c), 1)
    ltri = (ri >= ci).astype(BF16)
    k_mask = (lax.broadcasted_iota(jnp.int32, (hc, GLA_QK), 0) // c
              == lax.broadcasted_iota(jnp.int32, (hc, GLA_QK), 1) // GLA_DK)
    v_mask = (lax.broadcasted_iota(jnp.int32, (hc, GLA_V), 0) // c
              == lax.broadcasted_iota(jnp.int32, (hc, GLA_V), 1) // GLA_DV)
    s_mask = (lax.broadcasted_iota(jnp.int32, (GLA_V, GLA_QK), 0) // GLA_DV
              == lax.broadcasted_iota(jnp.int32, (GLA_V, GLA_QK), 1) // GLA_DK)
    causal = (lax.broadcasted_iota(jnp.int32, (c, hc), 1) % c
              <= lax.broadcasted_iota(jnp.int32, (c, hc), 0))
    gain = gain_ref[...]

    def chunk(n, carry):
        rows = pl.ds(pl.multiple_of(n * c, c), c)
        q = in_ref[0, rows, 0:GLA_QK]
        k = in_ref[0, rows, GLA_QK:2 * GLA_QK]
        la = in_ref[0, rows, 2 * GLA_QK:3 * GLA_QK]
        v = in_ref[0, rows, 3 * GLA_QK:3 * GLA_QK + GLA_V]
        gg = in_ref[0, rows, 3 * GLA_QK + GLA_V:]

        hi, mid, lo = _split3(la)
        cum = _dot(ltri, hi) + _dot(ltri, mid) + _dot(ltri, lo)
        cum_last = cum[c - 1:c, :]
        q_in = (q * (GLA_DK ** -0.5) * jnp.exp(cum)).astype(BF16)
        k_in = k * jnp.exp(-cum)
        k_out = (k * jnp.exp(cum_last - cum)).astype(BF16)
        vb = v.astype(BF16)

        k_bd = jnp.where(k_mask, jnp.concatenate([k_in] * GLA_HEADS, axis=0), 0.0).astype(BF16)
        attn = jnp.where(causal, _dot_nt(q_in, k_bd), 0.0).astype(BF16)
        v_bd = jnp.where(v_mask, jnp.concatenate([v] * GLA_HEADS, axis=0), 0.0).astype(BF16)
        st = st_ref[...]
        o = _dot(attn, v_bd) + _dot_nt(q_in, jnp.where(s_mask, st, 0.0).astype(BF16))
        st_ref[...] = st * jnp.exp(cum_last) + _dot_tn(vb, k_out)

        outs = []
        for h in range(GLA_HEADS):
            outs.append(_rms(o[:, h * GLA_DV:(h + 1) * GLA_DV]))
        y = jnp.concatenate(outs, axis=-1) * gain
        o_ref[0, rows, :] = (y * (gg * _sigmoid(gg))).astype(BF16)
        return carry

    lax.fori_loop(0, t2 // c, chunk, 0)


def _gla(gla_in, gain, *, t2):
    b, s, w = gla_in.shape
    return pl.pallas_call(
        functools.partial(_gla_kernel, t2=t2),
        grid=(b, s // t2),
        in_specs=[
            pl.BlockSpec((1, t2, w), lambda i, j: (i, j, 0)),
            pl.BlockSpec((1, GLA_V), lambda i, j: (0, 0)),
        ],
        out_specs=pl.BlockSpec((1, t2, GLA_V), lambda i, j: (i, j, 0)),
        out_shape=jax.ShapeDtypeStruct((b, s, GLA_V), BF16),
        scratch_shapes=[pltpu.VMEM((GLA_V, GLA_QK), F32)],
        compiler_params=_params(("arbitrary", "arbitrary")),
        name="gla",
    )(gla_in, gain)


def _fox_kernel(q_ref, k_ref, v_ref, cum_ref, o_ref, *, tq):
    i = pl.program_id(2)
    q = q_ref[0]
    q0 = pl.multiple_of(i * tq, tq)
    cq = cum_ref[0, 0, :, pl.ds(q0, tq)].T[:, 0:1]

    def scores(j):
        k0 = pl.multiple_of(j * tq, tq)
        s = _dot_nt(q, k_ref[0, pl.ds(k0, tq), :])
        return s + cq - cum_ref[0, 0, 0:1, pl.ds(k0, tq)], k0

    def update(s, k0, carry):
        m, l, acc = carry
        m_new = jnp.maximum(m, jnp.max(s, axis=-1, keepdims=True))
        p = jnp.exp(s - m_new)
        alpha = jnp.exp(m - m_new)
        l = alpha * l + jnp.sum(p, axis=-1, keepdims=True)
        acc = alpha * acc + _dot(p.astype(BF16), v_ref[0, pl.ds(k0, tq), :])
        return m_new, l, acc

    def body(j, carry):
        s, k0 = scores(j)
        return update(s, k0, carry)

    init = (jnp.full((tq, 1), -jnp.inf, F32), jnp.zeros((tq, 1), F32), jnp.zeros((tq, FOX_HD), F32))
    carry = lax.fori_loop(0, i, body, init)
    s, k0 = scores(i)
    keep = lax.broadcasted_iota(jnp.int32, (tq, tq), 1) <= lax.broadcasted_iota(jnp.int32, (tq, tq), 0)
    _, l, acc = update(jnp.where(keep, s, -jnp.inf), k0, carry)
    o_ref[0] = (acc / l).astype(o_ref.dtype)


def _fox(fq, fk, fv, cum, *, tq):
    b, s, _ = fq.shape
    return pl.pallas_call(
        functools.partial(_fox_kernel, tq=tq),
        grid=(b, FOX_HEADS, s // tq),
        in_specs=[
            pl.BlockSpec((1, tq, FOX_HD), lambda bi, h, i: (bi, i, h)),
            pl.BlockSpec((1, s, FOX_HD), lambda bi, h, i: (bi, 0, h)),
            pl.BlockSpec((1, s, FOX_HD), lambda bi, h, i: (bi, 0, h)),
            pl.BlockSpec((1, 1, SUBLANES, s), lambda bi, h, i: (bi, h, 0, 0)),
        ],
        out_specs=pl.BlockSpec((1, tq, FOX_HD), lambda bi, h, i: (bi, i, h)),
        out_shape=jax.ShapeDtypeStruct((b, s, FOX_W), BF16),
        compiler_params=_params(("arbitrary", "arbitrary", "arbitrary")),
        name="fox",
    )(fq, fk, fv, cum)


def _merge_kernel(x_ref, g_ref, wg_ref, bg_ref, yg_ref, yf_ref, mq_ref, mk_ref, mv_ref,
                  wb_ref, wo_ref, o_ref):
    x = x_ref[0]
    h = (_rms(x) * g_ref[...]).astype(BF16)

    outs = []
    for hd in range(MEM_HEADS):
        cols = slice(hd * MEM_HD, (hd + 1) * MEM_HD)
        s = _dot_nt(mq_ref[0, :, cols], mk_ref[0, 0, :, cols])
        p = jnp.exp(s - jnp.max(s, axis=-1, keepdims=True))
        p = p / jnp.sum(p, axis=-1, keepdims=True)
        outs.append(_dot(p.astype(BF16), mv_ref[0, 0, :, cols]))
    y_mem = jnp.concatenate(outs, axis=-1).astype(BF16)

    d = x.shape[-1]
    merged = None
    for i, y in enumerate((yg_ref[0], yf_ref[0], y_mem)):
        gate = _sigmoid(_dot(h, wg_ref[:, i * d:(i + 1) * d]) + bg_ref[:, i * d:(i + 1) * d])
        term = gate * _dot(y, wb_ref[i])
        merged = term if merged is None else merged + term
    o_ref[0] = x + _dot(merged.astype(BF16), wo_ref[...])


def _merge(x, g_mix, w_gate, b_gate, y_gla, y_fox, mq, mk, mv, w_branch, w_out, *, layer, tm):
    b, s, d = x.shape
    m = mk.shape[2]
    const = lambda i, j: (0, 0)
    tok = lambda i, j: (i, j, 0)
    return pl.pallas_call(
        _merge_kernel,
        grid=(b, s // tm),
        in_specs=[
            pl.BlockSpec((1, tm, d), tok),
            pl.BlockSpec((1, d), const),
            pl.BlockSpec((d, N_BRANCH * d), const),
            pl.BlockSpec((1, N_BRANCH * d), const),
            pl.BlockSpec((1, tm, GLA_V), tok),
            pl.BlockSpec((1, tm, FOX_W), tok),
            pl.BlockSpec((1, tm, MEM_W), tok),
            pl.BlockSpec((1, 1, m, MEM_W), lambda i, j: (layer, i, 0, 0)),
            pl.BlockSpec((1, 1, m, MEM_W), lambda i, j: (layer, i, 0, 0)),
            pl.BlockSpec((N_BRANCH, GLA_V, d), lambda i, j: (0, 0, 0)),
            pl.BlockSpec((d, d), const),
        ],
        out_specs=pl.BlockSpec((1, tm, d), tok),
        out_shape=jax.ShapeDtypeStruct((b, s, d), F32),
        compiler_params=_params(("arbitrary", "arbitrary")),
        name="merge",
    )(x, g_mix, w_gate, b_gate, y_gla, y_fox, mq, mk, mv, w_branch, w_out)


def _ffn_kernel(x_ref, g_ref, wg_ref, wu_ref, wd_ref, o_ref):
    x = x_ref[0]
    h = (_rms(x) * g_ref[...]).astype(BF16)
    gate = _dot(h, wg_ref[...])
    act = (gate * _sigmoid(gate) * _dot(h, wu_ref[...])).astype(BF16)
    o_ref[0] = x + _dot(act, wd_ref[...])


def _ffn(x, g_ffn, w_gate, w_up, w_down, *, tm):
    b, s, d = x.shape
    f = w_gate.shape[1]
    const = lambda i, j: (0, 0)
    tok = lambda i, j: (i, j, 0)
    once = dict(pipeline_mode=pl.Buffered(1))
    return pl.pallas_call(
        _ffn_kernel,
        grid=(b, s // tm),
        in_specs=[
            pl.BlockSpec((1, tm, d), tok),
            pl.BlockSpec((1, d), const),
            pl.BlockSpec((d, f), const, **once),
            pl.BlockSpec((d, f), const, **once),
            pl.BlockSpec((f, d), const, **once),
        ],
        out_specs=pl.BlockSpec((1, tm, d), tok),
        out_shape=jax.ShapeDtypeStruct((b, s, d), F32),
        compiler_params=_params(("arbitrary", "arbitrary")),
        name="ffn",
    )(x, g_ffn, w_gate, w_up, w_down)


def kernel(x, mem, g_mix, w_in, w_gla_a2, b_gla_a, g_gla_out, b_fox_f, g_fox_q, g_fox_k, g_mem, w_mem_kv, g_mem_q, g_mem_k, b_gate, w_branch, w_out, g_ffn, w_ffn_gate, w_ffn_up, w_ffn_down):
    depth, d = g_mix.shape
    n_main = 2 * GLA_QK + 2 * GLA_V
    o_ga = n_main
    o_fox = o_ga + GLA_RANK
    o_ff = o_fox + 3 * FOX_W
    o_mq = o_ff + FOX_HEADS
    o_bg = o_mq + MEM_W

    mk, mv = _memkv(mem, g_mem.reshape(depth, 1, d), w_mem_kv.astype(BF16),
                    g_mem_k.reshape(depth, 1, MEM_HD))

    for l in range(depth):
        w = w_in[l]
        small = jnp.zeros((d, SMALL_W), F32)
        small = small.at[:, SMALL_FF:SMALL_FF + FOX_HEADS].set(w[:, o_ff:o_mq])
        small = small.at[:, SMALL_GA:SMALL_GA + GLA_RANK].set(w[:, o_ga:o_fox])
        w_a = jnp.concatenate([w[:, :n_main], w[:, o_fox:o_ff], w[:, o_mq:o_bg], small], axis=1).astype(BF16)
        w2p = jnp.zeros((SMALL_W, GLA_QK), F32).at[SMALL_GA:SMALL_GA + GLA_RANK].set(w_gla_a2[l]).astype(BF16)
        b_small = jnp.zeros((1, SMALL_W), F32).at[0, SMALL_FF:SMALL_FF + FOX_HEADS].set(b_fox_f[l])

        gla_in, fq, fk, fv, mq, cum = _proj(
            x, g_mix[l].reshape(1, d), w_a, w2p, b_gla_a[l].reshape(1, GLA_QK), b_small,
            g_fox_q[l].reshape(1, FOX_HD), g_fox_k[l].reshape(1, FOX_HD), g_mem_q[l].reshape(1, MEM_HD), tm=512)
        y_gla = _gla(gla_in, g_gla_out[l].reshape(1, GLA_V), t2=512)
        y_fox = _fox(fq, fk, fv, cum, tq=512)
        x = _merge(x, g_mix[l].reshape(1, d), w[:, o_bg:].astype(BF16), b_gate[l].reshape(1, N_BRANCH * d),
                   y_gla, y_fox, mq, mk, mv, w_branch[l].astype(BF16), w_out[l].astype(BF16), layer=l, tm=512)
        x = _ffn(x, g_ffn[l].reshape(1, d), w_ffn_gate[l].astype(BF16), w_ffn_up[l].astype(BF16),
                 w_ffn_down[l].astype(BF16), tm=512)
    return x
```

```python
import functools

import jax
import jax.numpy as jnp
from jax import lax
from jax.experimental import pallas as pl
from jax.experimental.pallas import tpu as pltpu

EPS = 1e-6
GLA_HEADS = 4
GLA_DK = 64
GLA_DV = 128
GLA_RANK = 16
GLA_TAU = 16.0
GLA_CHUNK = 64
GLA_QK = GLA_HEADS * GLA_DK
GLA_V = GLA_HEADS * GLA_DV
FOX_HEADS = 4
FOX_HD = 128
FOX_W = FOX_HEADS * FOX_HD
MEM_HEADS = 4
MEM_HD = 128
MEM_W = MEM_HEADS * MEM_HD
N_BRANCH = 3

LANES = 128
SUBLANES = 8
VMEM_LIMIT_BYTES = 60000 * 1024

SMALL_W = LANES
SMALL_FF = 0
SMALL_GA = SUBLANES
GLA_IN_W = 3 * GLA_QK + 2 * GLA_V

BF16 = jnp.bfloat16
F32 = jnp.float32


def _dot(a, b):
    return jnp.dot(a, b, preferred_element_type=F32)


def _dot_nt(a, b):
    return lax.dot_general(a, b, (((1,), (1,)), ((), ())), preferred_element_type=F32)


def _dot_tn(a, b):
    return lax.dot_general(a, b, (((0,), (0,)), ((), ())), preferred_element_type=F32)


def _rms(x):
    return x * lax.rsqrt(jnp.mean(x * x, axis=-1, keepdims=True) + EPS)


def _log_sigmoid(x):
    return jnp.minimum(x, 0.0) - jnp.log1p(jnp.exp(-jnp.abs(x)))


def _sigmoid(x):
    return 1.0 / (1.0 + jnp.exp(-x))


def _head_rms(x, gain, heads, width):
    outs = []
    for h in range(heads):
        outs.append(_rms(x[:, h * width:(h + 1) * width]) * gain)
    return jnp.concatenate(outs, axis=-1)


def _params(sem):
    return pltpu.CompilerParams(dimension_semantics=sem, vmem_limit_bytes=VMEM_LIMIT_BYTES)


def _memkv_kernel(mem_ref, g_ref, w_ref, gk_ref, k_ref, v_ref):
    h = (_rms(mem_ref[0]) * g_ref[0]).astype(BF16)
    kv = _dot(h, w_ref[0])
    k_ref[0, 0] = _head_rms(kv[:, :MEM_W], gk_ref[0], MEM_HEADS, MEM_HD).astype(BF16)
    v_ref[0, 0] = kv[:, MEM_W:].astype(BF16)


def _memkv(mem, g_mem, w_kv, g_k):
    b, m, d = mem.shape
    depth = w_kv.shape[0]
    out = jax.ShapeDtypeStruct((depth, b, m, MEM_W), BF16)
    return pl.pallas_call(
        _memkv_kernel,
        grid=(depth, b),
        in_specs=[
            pl.BlockSpec((1, m, d), lambda l, i: (i, 0, 0)),
            pl.BlockSpec((1, 1, d), lambda l, i: (l, 0, 0)),
            pl.BlockSpec((1, d, 2 * MEM_W), lambda l, i: (l, 0, 0)),
            pl.BlockSpec((1, 1, MEM_HD), lambda l, i: (l, 0, 0)),
        ],
        out_specs=[
            pl.BlockSpec((1, 1, m, MEM_W), lambda l, i: (l, i, 0, 0)),
            pl.BlockSpec((1, 1, m, MEM_W), lambda l, i: (l, i, 0, 0)),
        ],
        out_shape=[out, out],
        compiler_params=_params(("arbitrary", "arbitrary")),
        name="memkv",
    )(mem, g_mem, w_kv, g_k)


def _wprep_kernel(w_ref, wa_ref, wg_ref, *, offs):
    o_ga, o_fox, o_ff, o_mq, o_bg, width = offs
    n_main = o_ga
    w = w_ref[0]
    rows = w.shape[0]
    wa_ref[0, :, 0:n_main] = w[:, 0:n_main].astype(BF16)
    wa_ref[0, :, n_main:n_main + 3 * FOX_W] = w[:, o_fox:o_ff].astype(BF16)
    wa_ref[0, :, n_main + 3 * FOX_W:n_main + 3 * FOX_W + MEM_W] = w[:, o_mq:o_bg].astype(BF16)
    small = jnp.concatenate([
        w[:, o_ff:o_mq], jnp.zeros((rows, SMALL_GA - FOX_HEADS), F32),
        w[:, o_ga:o_fox], jnp.zeros((rows, SMALL_W - SMALL_GA - GLA_RANK), F32)], axis=1)
    wa_ref[0, :, n_main + 3 * FOX_W + MEM_W:] = small.astype(BF16)
    wg_ref[0] = w[:, o_bg:width].astype(BF16)


def _wprep(w_in, offs, *, rb):
    depth, d, width = w_in.shape
    o_ga, o_fox, o_ff, o_mq, o_bg = offs
    wa = o_ga + 3 * FOX_W + MEM_W + SMALL_W
    wg = width - o_bg
    return pl.pallas_call(
        functools.partial(_wprep_kernel, offs=offs + (width,)),
        grid=(depth, d // rb),
        in_specs=[pl.BlockSpec((1, rb, width), lambda l, r: (l, r, 0))],
        out_specs=[pl.BlockSpec((1, rb, wa), lambda l, r: (l, r, 0)),
                   pl.BlockSpec((1, rb, wg), lambda l, r: (l, r, 0))],
        out_shape=[jax.ShapeDtypeStruct((depth, d, wa), BF16), jax.ShapeDtypeStruct((depth, d, wg), BF16)],
        compiler_params=_params(("arbitrary", "arbitrary")),
        name="wprep",
    )(w_in)


def _lane_cumsum(x):
    lane = lax.broadcasted_iota(jnp.int32, x.shape, 1)
    shift = 1
    while shift < LANES:
        x = x + jnp.where(lane >= shift, pltpu.roll(x, shift, 1), 0.0)
        shift *= 2
    return x


def _proj_kernel(x_ref, g_ref, w_ref, w2_ref, ba_ref, bs_ref, gfq_ref, gfk_ref, gmq_ref,
                 gla_ref, fq_ref, fk_ref, fv_ref, mq_ref, cum_ref, carry_ref, *, tm):
    @pl.when(pl.program_id(1) == 0)
    def _():
        carry_ref[...] = jnp.zeros_like(carry_ref)

    h = (_rms(x_ref[0]) * g_ref[...]).astype(BF16)

    o_gla, o_fox, o_mq, o_small = 0, 2 * GLA_QK + 2 * GLA_V, 2 * GLA_QK + 2 * GLA_V + 3 * FOX_W, \
        2 * GLA_QK + 2 * GLA_V + 3 * FOX_W + MEM_W

    small = _dot(h, w_ref[:, o_small:o_small + SMALL_W])
    log_a = _log_sigmoid(_dot(small.astype(BF16), w2_ref[...]) + ba_ref[...]) / GLA_TAU

    p = _dot(h, w_ref[:, o_gla:o_gla + 2 * GLA_QK])
    gla_ref[0, :, 0:2 * GLA_QK] = p
    gla_ref[0, :, 2 * GLA_QK:3 * GLA_QK] = log_a
    gla_ref[0, :, 3 * GLA_QK:] = _dot(h, w_ref[:, o_gla + 2 * GLA_QK:o_fox])

    p = _dot(h, w_ref[:, o_fox:o_fox + FOX_W])
    fq_ref[0] = (_head_rms(p, gfq_ref[...], FOX_HEADS, FOX_HD) * (FOX_HD ** -0.5)).astype(BF16)
    p = _dot(h, w_ref[:, o_fox + FOX_W:o_fox + 2 * FOX_W])
    fk_ref[0] = _head_rms(p, gfk_ref[...], FOX_HEADS, FOX_HD).astype(BF16)
    fv_ref[0] = _dot(h, w_ref[:, o_fox + 2 * FOX_W:o_mq]).astype(BF16)
    p = _dot(h, w_ref[:, o_mq:o_small])
    mq_ref[0] = (_head_rms(p, gmq_ref[...], MEM_HEADS, MEM_HD) * (MEM_HD ** -0.5)).astype(BF16)

    log_f = _log_sigmoid(small + bs_ref[...])
    log_ft = log_f.T[0:SUBLANES, :]
    carry = carry_ref[...]
    blocks = []
    for j in range(tm // LANES):
        c = _lane_cumsum(log_ft[:, j * LANES:(j + 1) * LANES]) + carry
        blocks.append(c)
        carry = jnp.broadcast_to(c[:, LANES - 1:LANES], carry.shape)
    carry_ref[...] = carry
    cum = jnp.concatenate(blocks, axis=-1)
    for hd in range(FOX_HEADS):
        cum_ref[0, hd] = jnp.broadcast_to(cum[hd:hd + 1, :], (SUBLANES, tm))


def _proj(x, g_mix, w_a, w2p, b_a, b_small, g_fq, g_fk, g_mq, *, tm):
    b, s, d = x.shape
    wa = w_a.shape[1]
    const = lambda i, j: (0, 0)
    tok = lambda i, j: (i, j, 0)
    return pl.pallas_call(
        functools.partial(_proj_kernel, tm=tm),
        grid=(b, s // tm),
        in_specs=[
            pl.BlockSpec((1, tm, d), tok),
            pl.BlockSpec((1, d), const),
            pl.BlockSpec((d, wa), const),
            pl.BlockSpec((SMALL_W, GLA_QK), const),
            pl.BlockSpec((1, GLA_QK), const),
            pl.BlockSpec((1, SMALL_W), const),
            pl.BlockSpec((1, FOX_HD), const),
            pl.BlockSpec((1, FOX_HD), const),
            pl.BlockSpec((1, MEM_HD), const),
        ],
        out_specs=[
            pl.BlockSpec((1, tm, GLA_IN_W), tok),
            pl.BlockSpec((1, tm, FOX_W), tok),
            pl.BlockSpec((1, tm, FOX_W), tok),
            pl.BlockSpec((1, tm, FOX_W), tok),
            pl.BlockSpec((1, tm, MEM_W), tok),
            pl.BlockSpec((1, FOX_HEADS, SUBLANES, tm), lambda i, j: (i, 0, 0, j)),
        ],
        out_shape=[
            jax.ShapeDtypeStruct((b, s, GLA_IN_W), F32),
            jax.ShapeDtypeStruct((b, s, FOX_W), BF16),
            jax.ShapeDtypeStruct((b, s, FOX_W), BF16),
            jax.ShapeDtypeStruct((b, s, FOX_W), BF16),
            jax.ShapeDtypeStruct((b, s, MEM_W), BF16),
            jax.ShapeDtypeStruct((b, FOX_HEADS, SUBLANES, s), F32),
        ],
        scratch_shapes=[pltpu.VMEM((SUBLANES, LANES), F32)],
        compiler_params=_params(("arbitrary", "arbitrary")),
        name="proj",
    )(x, g_mix, w_a, w2p, b_a, b_small, g_fq, g_fk, g_mq)


def _split3(x):
    hi = x.astype(BF16)
    r = x - hi.astype(F32)
    mid = r.astype(BF16)
    lo = (r - mid.astype(F32)).astype(BF16)
    return hi, mid, lo


GLA_SCAN = 256


def _gla_kernel(in_ref, gain_ref, lt_ref, km_ref, vm_ref, o_ref, st_ref, *, t2):
    @pl.when(pl.program_id(1) == 0)
    def _():
        st_ref[...] = jnp.zeros_like(st_ref)

    c = GLA_CHUNK
    nc = t2 // c
    hc = GLA_HEADS * c
    q = in_ref[0, :, 0:GLA_QK]
    k = in_ref[0, :, GLA_QK:2 * GLA_QK]
    la = in_ref[0, :, 2 * GLA_QK:3 * GLA_QK]
    v = in_ref[0, :, 3 * GLA_QK:3 * GLA_QK + GLA_V]
    gg = in_ref[0, :, 3 * GLA_QK + GLA_V:]

    lt = lt_ref[...]
    parts = []
    for r in range(t2 // GLA_SCAN):
        hi, mid, lo = _split3(la[r * GLA_SCAN:(r + 1) * GLA_SCAN])
        parts.append(_dot(lt, hi) + _dot(lt, mid) + _dot(lt, lo))
    cum = jnp.concatenate(parts, axis=0)
    cum_last = jnp.broadcast_to(cum.reshape(nc, c, GLA_QK)[:, c - 1:c, :], (nc, c, GLA_QK)).reshape(t2, GLA_QK)

    q_in = (q * (GLA_DK ** -0.5) * jnp.exp(cum)).astype(BF16)
    k_in = (k * jnp.exp(-cum)).astype(BF16)
    k_out = (k * jnp.exp(cum_last - cum)).astype(BF16)
    decay = jnp.exp(cum_last)
    vb = v.astype(BF16)
    km = km_ref[...]
    vm = vm_ref[...]
    causal = (lax.broadcasted_iota(jnp.int32, (c, hc), 1) % c
              <= lax.broadcasted_iota(jnp.int32, (c, hc), 0))

    st = st_ref[...]
    outs = []
    for n in range(nc):
        rows = slice(n * c, (n + 1) * c)
        qn = q_in[rows]
        k_bd = jnp.concatenate([k_in[rows]] * GLA_HEADS, axis=0) * km
        v_bd = jnp.concatenate([vb[rows]] * GLA_HEADS, axis=0) * vm
        ko_bd = jnp.concatenate([k_out[rows]] * GLA_HEADS, axis=0) * km
        attn = jnp.where(causal, _dot_nt(qn, k_bd), 0.0).astype(BF16)
        outs.append(_dot(attn, v_bd) + _dot_nt(qn, st.astype(BF16)))
        st = st * decay[n * c:n * c + 1] + _dot_tn(v_bd, ko_bd)
    st_ref[...] = st
    o = jnp.concatenate(outs, axis=0)

    y = jnp.concatenate([_rms(o[:, h * GLA_DV:(h + 1) * GLA_DV]) for h in range(GLA_HEADS)], axis=-1)
    o_ref[0] = (y * gain_ref[...] * (gg * _sigmoid(gg))).astype(BF16)


def _gla(gla_in, gain, *, t2):
    b, s, w = gla_in.shape
    c = GLA_CHUNK
    hc = GLA_HEADS * c
    r = jnp.arange(GLA_SCAN)
    lt = ((r[:, None] // c == r[None, :] // c) & (r[:, None] >= r[None, :])).astype(BF16)
    rh = jnp.arange(hc) // c
    km = (rh[:, None] == jnp.arange(GLA_QK)[None, :] // GLA_DK).astype(BF16)
    vm = (rh[:, None] == jnp.arange(GLA_V)[None, :] // GLA_DV).astype(BF16)
    const = lambda i, j: (0, 0)
    return pl.pallas_call(
        functools.partial(_gla_kernel, t2=t2),
        grid=(b, s // t2),
        in_specs=[
            pl.BlockSpec((1, t2, w), lambda i, j: (i, j, 0)),
            pl.BlockSpec((1, GLA_V), const),
            pl.BlockSpec((GLA_SCAN, GLA_SCAN), const),
            pl.BlockSpec((hc, GLA_QK), const),
            pl.BlockSpec((hc, GLA_V), const),
        ],
        out_specs=pl.BlockSpec((1, t2, GLA_V), lambda i, j: (i, j, 0)),
        out_shape=jax.ShapeDtypeStruct((b, s, GLA_V), BF16),
        scratch_shapes=[pltpu.VMEM((GLA_V, GLA_QK), F32)],
        compiler_params=_params(("arbitrary", "arbitrary")),
        name="gla",
    )(gla_in, gain, lt, km, vm)


FOX_ZERO_EXP = 106.0


def _fox_kernel(js_ref, q_ref, k_ref, v_ref, cum_ref, o_ref, *, tq, nq):
    bh = pl.program_id(0) * FOX_HEADS + pl.program_id(1)
    keep = lax.broadcasted_iota(jnp.int32, (tq, tq), 1) <= lax.broadcasted_iota(jnp.int32, (tq, tq), 0)

    def q_block(i, _):
        q0 = pl.multiple_of(i * tq, tq)
        q = q_ref[0, pl.ds(q0, tq), :]
        cq = cum_ref[0, 0, :, pl.ds(q0, tq)].T[:, 0:1]

        def scores(j):
            k0 = pl.multiple_of(j * tq, tq)
            s = _dot_nt(q, k_ref[0, pl.ds(k0, tq), :])
            return s + cq - cum_ref[0, 0, 0:1, pl.ds(k0, tq)], k0

        def update(s, k0, carry):
            m, l, acc = carry
            m_new = jnp.maximum(m, jnp.max(s, axis=-1, keepdims=True))
            p = jnp.exp(s - m_new)
            alpha = jnp.exp(m - m_new)
            l = alpha * l + jnp.sum(p, axis=-1, keepdims=True)
            acc = alpha * acc + _dot(p.astype(BF16), v_ref[0, pl.ds(k0, tq), :])
            return m_new, l, acc

        def body(j, carry):
            s, k0 = scores(j)
            return update(s, k0, carry)

        init = (jnp.full((tq, 1), -jnp.inf, F32), jnp.zeros((tq, 1), F32), jnp.zeros((tq, FOX_HD), F32))
        carry = lax.fori_loop(js_ref[bh * nq + i], i, body, init)
        s, k0 = scores(i)
        _, l, acc = update(jnp.where(keep, s, -jnp.inf), k0, carry)
        o_ref[0, pl.ds(q0, tq), :] = (acc / l).astype(o_ref.dtype)
        return 0

    lax.fori_loop(0, nq, q_block, 0)


def _fox_first_block(cum, g_q, g_k, tq):
    c = cum[:, :, 0, :]
    first = c[:, :, 0::tq]
    last = c[:, :, tq - 1::tq]
    nq = first.shape[-1]
    bound = 2.0 * 1.02 * (FOX_HD ** 0.5) * jnp.max(jnp.abs(g_q)) * jnp.max(jnp.abs(g_k))
    needed = first[..., :, None] - last[..., None, :] + bound >= -FOX_ZERO_EXP
    idx = jnp.arange(nq, dtype=jnp.int32)
    js = jnp.min(jnp.where(needed, idx, nq), axis=-1)
    return jnp.minimum(js, idx).reshape(-1).astype(jnp.int32)


def _fox(fq, fk, fv, cum, g_q, g_k, *, tq):
    b, s, _ = fq.shape
    nq = s // tq
    js = _fox_first_block(cum, g_q, g_k, tq)
    blk = lambda bi, h, js_ref: (bi, 0, h)
    grid_spec = pltpu.PrefetchScalarGridSpec(
        num_scalar_prefetch=1,
        grid=(b, FOX_HEADS),
        in_specs=[
            pl.BlockSpec((1, s, FOX_HD), blk),
            pl.BlockSpec((1, s, FOX_HD), blk),
            pl.BlockSpec((1, s, FOX_HD), blk),
            pl.BlockSpec((1, 1, SUBLANES, s), lambda bi, h, js_ref: (bi, h, 0, 0)),
        ],
        out_specs=pl.BlockSpec((1, s, FOX_HD), blk),
    )
    return pl.pallas_call(
        functools.partial(_fox_kernel, tq=tq, nq=nq),
        grid_spec=grid_spec,
        out_shape=jax.ShapeDtypeStruct((b, s, FOX_W), BF16),
        compiler_params=_params(("arbitrary", "arbitrary")),
        name="fox",
    )(js, fq, fk, fv, cum)


def _merge_kernel(x_ref, g_ref, wg_ref, bg_ref, yg_ref, yf_ref, mq_ref, mk_ref, mv_ref,
                  wb_ref, wo_ref, o_ref):
    x = x_ref[0]
    h = (_rms(x) * g_ref[...]).astype(BF16)

    outs = []
    for hd in range(MEM_HEADS):
        cols = slice(hd * MEM_HD, (hd + 1) * MEM_HD)
        s = _dot_nt(mq_ref[0, :, cols], mk_ref[0, 0, :, cols])
        p = jnp.exp(s - jnp.max(s, axis=-1, keepdims=True))
        p = p / jnp.sum(p, axis=-1, keepdims=True)
        outs.append(_dot(p.astype(BF16), mv_ref[0, 0, :, cols]))
    y_mem = jnp.concatenate(outs, axis=-1).astype(BF16)

    d = x.shape[-1]
    merged = None
    for i, y in enumerate((yg_ref[0], yf_ref[0], y_mem)):
        gate = _sigmoid(_dot(h, wg_ref[:, i * d:(i + 1) * d]) + bg_ref[:, i * d:(i + 1) * d])
        term = gate * _dot(y, wb_ref[i])
        merged = term if merged is None else merged + term
    o_ref[0] = x + _dot(merged.astype(BF16), wo_ref[...])


def _merge(x, g_mix, w_gate, b_gate, y_gla, y_fox, mq, mk, mv, w_branch, w_out, *, layer, tm):
    b, s, d = x.shape
    m = mk.shape[2]
    const = lambda i, j: (0, 0)
    tok = lambda i, j: (i, j, 0)
    return pl.pallas_call(
        _merge_kernel,
        grid=(b, s // tm),
        in_specs=[
            pl.BlockSpec((1, tm, d), tok),
            pl.BlockSpec((1, d), const),
            pl.BlockSpec((d, N_BRANCH * d), const),
            pl.BlockSpec((1, N_BRANCH * d), const),
            pl.BlockSpec((1, tm, GLA_V), tok),
            pl.BlockSpec((1, tm, FOX_W), tok),
            pl.BlockSpec((1, tm, MEM_W), tok),
            pl.BlockSpec((1, 1, m, MEM_W), lambda i, j: (layer, i, 0, 0)),
            pl.BlockSpec((1, 1, m, MEM_W), lambda i, j: (layer, i, 0, 0)),
            pl.BlockSpec((N_BRANCH, GLA_V, d), lambda i, j: (0, 0, 0)),
            pl.BlockSpec((d, d), const),
        ],
        out_specs=pl.BlockSpec((1, tm, d), tok),
        out_shape=jax.ShapeDtypeStruct((b, s, d), F32),
        compiler_params=_params(("arbitrary", "arbitrary")),
        name="merge",
    )(x, g_mix, w_gate, b_gate, y_gla, y_fox, mq, mk, mv, w_branch, w_out)


def _ffn_kernel(x_ref, g_ref, wg_ref, wu_ref, wd_ref, o_ref):
    x = x_ref[0]
    h = (_rms(x) * g_ref[...]).astype(BF16)
    gate = _dot(h, wg_ref[...])
    act = (gate * _sigmoid(gate) * _dot(h, wu_ref[...])).astype(BF16)
    o_ref[0] = x + _dot(act, wd_ref[...])


def _ffn(x, g_ffn, w_gate, w_up, w_down, *, tm):
    b, s, d = x.shape
    f = w_gate.shape[1]
    const = lambda i, j: (0, 0)
    tok = lambda i, j: (i, j, 0)
    once = dict(pipeline_mode=pl.Buffered(1))
    return pl.pallas_call(
        _ffn_kernel,
        grid=(b, s // tm),
        in_specs=[
            pl.BlockSpec((1, tm, d), tok),
            pl.BlockSpec((1, d), const),
            pl.BlockSpec((d, f), const, **once),
            pl.BlockSpec((d, f), const, **once),
            pl.BlockSpec((f, d), const, **once),
        ],
        out_specs=pl.BlockSpec((1, tm, d), tok),
        out_shape=jax.ShapeDtypeStruct((b, s, d), F32),
        compiler_params=_params(("arbitrary", "arbitrary")),
        name="ffn",
    )(x, g_ffn, w_gate, w_up, w_down)


def kernel(x, mem, g_mix, w_in, w_gla_a2, b_gla_a, g_gla_out, b_fox_f, g_fox_q, g_fox_k, g_mem, w_mem_kv, g_mem_q, g_mem_k, b_gate, w_branch, w_out, g_ffn, w_ffn_gate, w_ffn_up, w_ffn_down):
    depth, d = g_mix.shape
    o_ga = 2 * GLA_QK + 2 * GLA_V
    o_fox = o_ga + GLA_RANK
    o_ff = o_fox + 3 * FOX_W
    o_mq = o_ff + FOX_HEADS
    o_bg = o_mq + MEM_W

    mk, mv = _memkv(mem, g_mem.reshape(depth, 1, d), w_mem_kv.astype(BF16),
                    g_mem_k.reshape(depth, 1, MEM_HD))
    w_a, w_g = _wprep(w_in, (o_ga, o_fox, o_ff, o_mq, o_bg), rb=256)
    w2p = jnp.zeros((depth, SMALL_W, GLA_QK), F32).at[:, SMALL_GA:SMALL_GA + GLA_RANK].set(w_gla_a2).astype(BF16)
    b_small = jnp.zeros((depth, 1, SMALL_W), F32).at[:, 0, SMALL_FF:SMALL_FF + FOX_HEADS].set(b_fox_f)
    w_br, w_o = w_branch.astype(BF16), w_out.astype(BF16)
    w_fg, w_fu, w_fd = w_ffn_gate.astype(BF16), w_ffn_up.astype(BF16), w_ffn_down.astype(BF16)

    for l in range(depth):
        gla_in, fq, fk, fv, mq, cum = _proj(
            x, g_mix[l].reshape(1, d), w_a[l], w2p[l], b_gla_a[l].reshape(1, GLA_QK), b_small[l],
            g_fox_q[l].reshape(1, FOX_HD), g_fox_k[l].reshape(1, FOX_HD), g_mem_q[l].reshape(1, MEM_HD), tm=512)
        y_gla = _gla(gla_in, g_gla_out[l].reshape(1, GLA_V), t2=512)
        y_fox = _fox(fq, fk, fv, cum, g_fox_q[l], g_fox_k[l], tq=256)
        x = _merge(x, g_mix[l].reshape(1, d), w_g[l], b_gate[l].reshape(1, N_BRANCH * d),
                   y_gla, y_fox, mq, mk, mv, w_br[l], w_o[l], layer=l, tm=512)
        x = _ffn(x, g_ffn[l].reshape(1, d), w_fg[l], w_fu[l], w_fd[l], tm=512)
    return x
```

```python
import functools

import jax
import jax.numpy as jnp
from jax import lax
from jax.experimental import pallas as pl
from jax.experimental.pallas import tpu as pltpu

EPS = 1e-6
GLA_HEADS = 4
GLA_DK = 64
GLA_DV = 128
GLA_RANK = 16
GLA_TAU = 16.0
GLA_CHUNK = 64
GLA_QK = GLA_HEADS * GLA_DK
GLA_V = GLA_HEADS * GLA_DV
FOX_HEADS = 4
FOX_HD = 128
FOX_W = FOX_HEADS * FOX_HD
MEM_HEADS = 4
MEM_HD = 128
MEM_W = MEM_HEADS * MEM_HD
N_BRANCH = 3
LOG2E = 1.4426950408889634

LANES = 128
SUBLANES = 8
VMEM_LIMIT_BYTES = 60000 * 1024

SMALL_W = LANES
SMALL_FF = 0
SMALL_GA = SUBLANES
GLA_IN_W = 3 * GLA_QK + 2 * GLA_V

BF16 = jnp.bfloat16
F32 = jnp.float32


def _dot(a, b):
    return jnp.dot(a, b, preferred_element_type=F32)


def _dot_nt(a, b):
    return lax.dot_general(a, b, (((1,), (1,)), ((), ())), preferred_element_type=F32)


def _dot_tn(a, b):
    return lax.dot_general(a, b, (((0,), (0,)), ((), ())), preferred_element_type=F32)


def _rms(x):
    return x * lax.rsqrt(jnp.mean(x * x, axis=-1, keepdims=True) + EPS)


def _log_sigmoid(x):
    return jnp.minimum(x, 0.0) - jnp.log1p(jnp.exp(-jnp.abs(x)))


def _sigmoid(x):
    return 1.0 / (1.0 + jnp.exp(-x))


def _head_rms(x, gain, heads, width):
    outs = []
    for h in range(heads):
        outs.append(_rms(x[:, h * width:(h + 1) * width]) * gain)
    return jnp.concatenate(outs, axis=-1)


def _params(sem):
    return pltpu.CompilerParams(dimension_semantics=sem, vmem_limit_bytes=VMEM_LIMIT_BYTES)


def _layer_spec(arr, layer):
    zeros = (0,) * (arr.ndim - 1)
    return pl.BlockSpec((1,) + arr.shape[1:], lambda i, j: (layer,) + zeros, pipeline_mode=pl.Buffered(1))


def _run_staggered(tiles, lead):
    live = {k: t for k, t in enumerate(tiles)}
    step = 0
    while live:
        for k in sorted(live):
            if step >= k * lead and next(live[k], StopIteration) is StopIteration:
                del live[k]
        step += 1


def _rows(p):
    return p.reshape(p.shape[0], 1, p.shape[1])


def _memkv_kernel(mem_ref, g_ref, w_ref, gk_ref, k_ref, v_ref):
    h = (_rms(mem_ref[0]) * g_ref[0]).astype(BF16)
    kv = _dot(h, w_ref[0])
    k_ref[0, 0] = _head_rms(kv[:, :MEM_W], gk_ref[0], MEM_HEADS, MEM_HD).astype(BF16)
    v_ref[0, 0] = kv[:, MEM_W:].astype(BF16)


def _memkv(mem, g_mem, w_kv, g_k):
    b, m, d = mem.shape
    depth = w_kv.shape[0]
    out = jax.ShapeDtypeStruct((depth, b, m, MEM_W), BF16)
    return pl.pallas_call(
        _memkv_kernel,
        grid=(depth, b),
        in_specs=[
            pl.BlockSpec((1, m, d), lambda l, i: (i, 0, 0)),
            pl.BlockSpec((1, 1, d), lambda l, i: (l, 0, 0)),
            pl.BlockSpec((1, d, 2 * MEM_W), lambda l, i: (l, 0, 0)),
            pl.BlockSpec((1, 1, MEM_HD), lambda l, i: (l, 0, 0)),
        ],
        out_specs=[
            pl.BlockSpec((1, 1, m, MEM_W), lambda l, i: (l, i, 0, 0)),
            pl.BlockSpec((1, 1, m, MEM_W), lambda l, i: (l, i, 0, 0)),
        ],
        out_shape=[out, out],
        compiler_params=_params(("arbitrary", "arbitrary")),
        name="memkv",
    )(mem, g_mem, w_kv, g_k)


def _wprep_kernel(w_ref, wa_ref, wg_ref, *, offs):
    o_ga, o_fox, o_ff, o_mq, o_bg, width = offs
    n_main = o_ga
    w = w_ref[0]
    rows = w.shape[0]
    wa_ref[0, :, 0:n_main] = w[:, 0:n_main].astype(BF16)
    wa_ref[0, :, n_main:n_main + 3 * FOX_W] = w[:, o_fox:o_ff].astype(BF16)
    wa_ref[0, :, n_main + 3 * FOX_W:n_main + 3 * FOX_W + MEM_W] = w[:, o_mq:o_bg].astype(BF16)
    small = jnp.concatenate([
        w[:, o_ff:o_mq], jnp.zeros((rows, SMALL_GA - FOX_HEADS), F32),
        w[:, o_ga:o_fox], jnp.zeros((rows, SMALL_W - SMALL_GA - GLA_RANK), F32)], axis=1)
    wa_ref[0, :, n_main + 3 * FOX_W + MEM_W:] = small.astype(BF16)
    wg_ref[0] = w[:, o_bg:width].astype(BF16)


def _wprep(w_in, offs, *, rb):
    depth, d, width = w_in.shape
    o_ga, o_fox, o_ff, o_mq, o_bg = offs
    wa = o_ga + 3 * FOX_W + MEM_W + SMALL_W
    wg = width - o_bg
    return pl.pallas_call(
        functools.partial(_wprep_kernel, offs=offs + (width,)),
        grid=(depth, d // rb),
        in_specs=[pl.BlockSpec((1, rb, width), lambda l, r: (l, r, 0))],
        out_specs=[pl.BlockSpec((1, rb, wa), lambda l, r: (l, r, 0)),
                   pl.BlockSpec((1, rb, wg), lambda l, r: (l, r, 0))],
        out_shape=[jax.ShapeDtypeStruct((depth, d, wa), BF16), jax.ShapeDtypeStruct((depth, d, wg), BF16)],
        compiler_params=_params(("arbitrary", "arbitrary")),
        name="wprep",
    )(w_in)


def _lane_cumsum(x):
    lane = lax.broadcasted_iota(jnp.int32, x.shape, 1)
    shift = 1
    while shift < LANES:
        x = x + jnp.where(lane >= shift, pltpu.roll(x, shift, 1), 0.0)
        shift *= 2
    return x


def _proj_kernel(x_ref, g_ref, w_ref, w2_ref, ba_ref, bs_ref, gfq_ref, gfk_ref, gmq_ref,
                 gla_ref, fq_ref, fk_ref, fv_ref, mq_ref, cum_ref, carry_ref, *, tm, sub):
    @pl.when(pl.program_id(1) == 0)
    def _():
        carry_ref[...] = jnp.zeros_like(carry_ref)

    o_gla, o_fox, o_mq, o_small = 0, 2 * GLA_QK + 2 * GLA_V, 2 * GLA_QK + 2 * GLA_V + 3 * FOX_W, \
        2 * GLA_QK + 2 * GLA_V + 3 * FOX_W + MEM_W

    def sub_tile(r0, n):
        rows = slice(r0, r0 + n)
        h = (_rms(x_ref[0, rows, :]) * g_ref[0]).astype(BF16)
        yield
        small = _dot(h, w_ref[0, :, o_small:o_small + SMALL_W])
        p_qk = _dot(h, w_ref[0, :, o_gla:o_gla + 2 * GLA_QK])
        yield
        a_logit = _dot(small.astype(BF16), w2_ref[0])
        p_vg = _dot(h, w_ref[0, :, o_gla + 2 * GLA_QK:o_fox])
        gla_ref[0, rows, 0:2 * GLA_QK] = p_qk
        gla_ref[0, rows, 2 * GLA_QK:3 * GLA_QK] = _log_sigmoid(a_logit + ba_ref[0]) / GLA_TAU
        yield
        p_fq = _dot(h, w_ref[0, :, o_fox:o_fox + FOX_W])
        gla_ref[0, rows, 3 * GLA_QK:] = p_vg
        yield
        p_fk = _dot(h, w_ref[0, :, o_fox + FOX_W:o_fox + 2 * FOX_W])
        fq_ref[0, rows, :] = (_head_rms(p_fq, gfq_ref[0], FOX_HEADS, FOX_HD)
                              * (FOX_HD ** -0.5 * LOG2E)).astype(BF16)
        yield
        p_fv = _dot(h, w_ref[0, :, o_fox + 2 * FOX_W:o_mq])
        fk_ref[0, rows, :] = _head_rms(p_fk, gfk_ref[0], FOX_HEADS, FOX_HD).astype(BF16)
        yield
        p_mq = _dot(h, w_ref[0, :, o_mq:o_small])
        fv_ref[0, rows, :] = p_fv.astype(BF16)
        yield
        log_f = _log_sigmoid(small + bs_ref[0])
        log_ft = log_f.T[0:SUBLANES, :]
        carry = carry_ref[...]
        blocks = []
        for j in range(n // LANES):
            c = _lane_cumsum(log_ft[:, j * LANES:(j + 1) * LANES]) + carry
            blocks.append(c)
            carry = jnp.broadcast_to(c[:, LANES - 1:LANES], carry.shape)
        carry_ref[...] = carry
        cum = jnp.concatenate(blocks, axis=-1) * LOG2E
        for hd in range(FOX_HEADS):
            cum_ref[0, hd, :, rows] = jnp.broadcast_to(cum[hd:hd + 1, :], (SUBLANES, n))
        yield
        mq_ref[0, rows, :] = (_head_rms(p_mq, gmq_ref[0], MEM_HEADS, MEM_HD) * (MEM_HD ** -0.5)).astype(BF16)

    _run_staggered([sub_tile(r0, sub) for r0 in range(0, tm, sub)], lead=3)


def _proj(x, g_mix, w_a, w2p, b_a, b_small, g_fq, g_fk, g_mq, *, layer, tm, sub):
    b, s, d = x.shape
    tok = lambda i, j: (i, j, 0)
    stacked = (g_mix, w_a, w2p, b_a, b_small, g_fq, g_fk, g_mq)
    return pl.pallas_call(
        functools.partial(_proj_kernel, tm=tm, sub=sub),
        grid=(b, s // tm),
        in_specs=[pl.BlockSpec((1, tm, d), tok)] + [_layer_spec(p, layer) for p in stacked],
        out_specs=[
            pl.BlockSpec((1, tm, GLA_IN_W), tok),
            pl.BlockSpec((1, tm, FOX_W), tok),
            pl.BlockSpec((1, tm, FOX_W), tok),
            pl.BlockSpec((1, tm, FOX_W), tok),
            pl.BlockSpec((1, tm, MEM_W), tok),
            pl.BlockSpec((1, FOX_HEADS, SUBLANES, tm), lambda i, j: (i, 0, 0, j)),
        ],
        out_shape=[
            jax.ShapeDtypeStruct((b, s, GLA_IN_W), F32),
            jax.ShapeDtypeStruct((b, s, FOX_W), BF16),
            jax.ShapeDtypeStruct((b, s, FOX_W), BF16),
            jax.ShapeDtypeStruct((b, s, FOX_W), BF16),
            jax.ShapeDtypeStruct((b, s, MEM_W), BF16),
            jax.ShapeDtypeStruct((b, FOX_HEADS, SUBLANES, s), F32),
        ],
        scratch_shapes=[pltpu.VMEM((SUBLANES, LANES), F32)],
        compiler_params=_params(("arbitrary", "arbitrary")),
        name="proj",
    )(x, *stacked)


def _split3(x):
    hi = x.astype(BF16)
    r = x - hi.astype(F32)
    mid = r.astype(BF16)
    lo = (r - mid.astype(F32)).astype(BF16)
    return hi, mid, lo


GLA_SCAN = 256


def _gla_kernel(in_ref, gain_ref, lt_ref, km_ref, vm_ref, o_ref, st_ref, *, t2):
    @pl.when(pl.program_id(1) == 0)
    def _():
        st_ref[...] = jnp.zeros_like(st_ref)

    c = GLA_CHUNK
    nc = t2 // c
    hc = GLA_HEADS * c
    q = in_ref[0, :, 0:GLA_QK]
    k = in_ref[0, :, GLA_QK:2 * GLA_QK]
    la = in_ref[0, :, 2 * GLA_QK:3 * GLA_QK]
    v = in_ref[0, :, 3 * GLA_QK:3 * GLA_QK + GLA_V]
    gg = in_ref[0, :, 3 * GLA_QK + GLA_V:]

    lt = lt_ref[...]
    parts = []
    for r in range(t2 // GLA_SCAN):
        hi, mid, lo = _split3(la[r * GLA_SCAN:(r + 1) * GLA_SCAN])
        parts.append(_dot(lt, hi) + _dot(lt, mid) + _dot(lt, lo))
    cum = jnp.concatenate(parts, axis=0)
    cum_last = jnp.broadcast_to(cum.reshape(nc, c, GLA_QK)[:, c - 1:c, :], (nc, c, GLA_QK)).reshape(t2, GLA_QK)

    q_in = (q * (GLA_DK ** -0.5) * jnp.exp(cum)).astype(BF16)
    k_in = (k * jnp.exp(-cum)).astype(BF16)
    k_out = (k * jnp.exp(cum_last - cum)).astype(BF16)
    decay = jnp.exp(cum_last)
    vb = v.astype(BF16)
    km = km_ref[...]
    vm = vm_ref[...]
    causal = (lax.broadcasted_iota(jnp.int32, (c, hc), 1) % c
              <= lax.broadcasted_iota(jnp.int32, (c, hc), 0))

    intra, updates = [], []
    for n in range(nc):
        rows = slice(n * c, (n + 1) * c)
        k_bd = jnp.concatenate([k_in[rows]] * GLA_HEADS, axis=0) * km
        v_bd = jnp.concatenate([vb[rows]] * GLA_HEADS, axis=0) * vm
        ko_bd = jnp.concatenate([k_out[rows]] * GLA_HEADS, axis=0) * km
        attn = jnp.where(causal, _dot_nt(q_in[rows], k_bd), 0.0).astype(BF16)
        intra.append(_dot(attn, v_bd))
        updates.append(_dot_tn(v_bd, ko_bd))
    st = st_ref[...]
    outs = []
    for n in range(nc):
        outs.append(intra[n] + _dot_nt(q_in[n * c:(n + 1) * c], st.astype(BF16)))
        st = st * decay[n * c:n * c + 1] + updates[n]
    st_ref[...] = st
    o = jnp.concatenate(outs, axis=0)

    y = jnp.concatenate([_rms(o[:, h * GLA_DV:(h + 1) * GLA_DV]) for h in range(GLA_HEADS)], axis=-1)
    o_ref[0] = (y * gain_ref[0] * (gg * _sigmoid(gg))).astype(BF16)


def _gla(gla_in, gain, *, layer, t2):
    b, s, w = gla_in.shape
    c = GLA_CHUNK
    hc = GLA_HEADS * c
    r = jnp.arange(GLA_SCAN)
    lt = ((r[:, None] // c == r[None, :] // c) & (r[:, None] >= r[None, :])).astype(BF16)
    rh = jnp.arange(hc) // c
    km = (rh[:, None] == jnp.arange(GLA_QK)[None, :] // GLA_DK).astype(BF16)
    vm = (rh[:, None] == jnp.arange(GLA_V)[None, :] // GLA_DV).astype(BF16)
    const = lambda i, j: (0, 0)
    return pl.pallas_call(
        functools.partial(_gla_kernel, t2=t2),
        grid=(b, s // t2),
        in_specs=[
            pl.BlockSpec((1, t2, w), lambda i, j: (i, j, 0)),
            _layer_spec(gain, layer),
            pl.BlockSpec((GLA_SCAN, GLA_SCAN), const),
            pl.BlockSpec((hc, GLA_QK), const),
            pl.BlockSpec((hc, GLA_V), const),
        ],
        out_specs=pl.BlockSpec((1, t2, GLA_V), lambda i, j: (i, j, 0)),
        out_shape=jax.ShapeDtypeStruct((b, s, GLA_V), BF16),
        scratch_shapes=[pltpu.VMEM((GLA_V, GLA_QK), F32)],
        compiler_params=_params(("arbitrary", "arbitrary")),
        name="gla",
    )(gla_in, gain, lt, km, vm)


FOX_ZERO_EXP = 106.0
FOX_GROUP = 8


def _fox_kernel(js_ref, q_ref, k_ref, v_ref, cum_ref, o_ref, m_ref, l_ref, acc_ref, *, tq, tk, nq):
    bh = pl.program_id(0) * FOX_HEADS + pl.program_id(1)
    win = tk + tq
    rel = (lax.broadcasted_iota(jnp.int32, (tq, win), 1)
           - lax.broadcasted_iota(jnp.int32, (tq, win), 0))

    def load_q(i):
        q0 = pl.multiple_of(i * tq, tq)
        q = q_ref[0, pl.ds(q0, tq), :]
        cq = cum_ref[0, 0, :, pl.ds(q0, tq)].T[:, 0:1]
        return q0, q, cq

    def window_scores(i):
        q0, q, cq = load_q(i)
        k0 = pl.multiple_of(jnp.maximum(q0 - tk, 0), tk)
        s = _dot_nt(q, k_ref[0, pl.ds(k0, win), :]) + cq - cum_ref[0, 0, 0:1, pl.ds(k0, win)]
        return jnp.where(rel <= q0 - k0, s, -jnp.inf), k0

    def window_softmax(slot, s, k0):
        m = jnp.max(s, axis=-1, keepdims=True)
        p = jnp.exp2(s - m)
        m_ref[slot] = m
        l_ref[slot] = jnp.sum(p, axis=-1, keepdims=True)
        acc_ref[slot] = _dot(p.astype(BF16), v_ref[0, pl.ds(k0, win), :])

    def earlier(slot, i):
        first = js_ref[bh * nq + i]
        last = i * (tq // tk) - 1

        @pl.when(first < last)
        def _():
            _, q, cq = load_q(i)

            def body(j, _):
                k0 = pl.multiple_of(j * tk, tk)
                s = _dot_nt(q, k_ref[0, pl.ds(k0, tk), :]) + cq - cum_ref[0, 0, 0:1, pl.ds(k0, tk)]
                m = m_ref[slot]
                m_new = jnp.maximum(m, jnp.max(s, axis=-1, keepdims=True))
                p = jnp.exp2(s - m_new)
                alpha = jnp.exp2(m - m_new)
                l_ref[slot] = alpha * l_ref[slot] + jnp.sum(p, axis=-1, keepdims=True)
                acc_ref[slot] = alpha * acc_ref[slot] + _dot(p.astype(BF16), v_ref[0, pl.ds(k0, tk), :])
                m_ref[slot] = m_new
                return 0

            lax.fori_loop(first, last, body, 0)

    def finish(slot, i):
        q0 = pl.multiple_of(i * tq, tq)
        o_ref[0, pl.ds(q0, tq), :] = (acc_ref[slot] / l_ref[slot]).astype(o_ref.dtype)

    def step(n, _):
        blocks = [n * FOX_GROUP + slot for slot in range(FOX_GROUP)]
        scores = [window_scores(i) for i in blocks]
        for slot, (s, k0) in enumerate(scores):
            window_softmax(slot, s, k0)
        for slot, i in enumerate(blocks):
            earlier(slot, i)
        for slot, i in enumerate(blocks):
            finish(slot, i)
        return 0

    lax.fori_loop(0, nq // FOX_GROUP, step, 0)


def _fox_first_block(cum, g_q, g_k, tq, tk):
    c = cum[:, :, 0, :] / LOG2E
    first = c[:, :, 0::tq]
    last = c[:, :, tk - 1::tk]
    nq, nk = first.shape[-1], last.shape[-1]
    bound = 2.0 * 1.02 * (FOX_HD ** 0.5) * jnp.max(jnp.abs(g_q)) * jnp.max(jnp.abs(g_k))
    needed = first[..., :, None] - last[..., None, :] + bound >= -FOX_ZERO_EXP
    js = jnp.min(jnp.where(needed, jnp.arange(nk, dtype=jnp.int32), nk), axis=-1)
    window_start = jnp.maximum(jnp.arange(nq, dtype=jnp.int32) * (tq // tk) - 1, 0)
    return jnp.minimum(js, window_start).reshape(-1).astype(jnp.int32)


def _fox(fq, fk, fv, cum, g_q, g_k, *, tq, tk):
    b, s, _ = fq.shape
    nq = s // tq
    assert nq % FOX_GROUP == 0 and tq % tk == 0 and s >= tq + tk
    js = _fox_first_block(cum, g_q, g_k, tq, tk)
    blk = lambda bi, h, js_ref: (bi, 0, h)
    grid_spec = pltpu.PrefetchScalarGridSpec(
        num_scalar_prefetch=1,
        grid=(b, FOX_HEADS),
        in_specs=[
            pl.BlockSpec((1, s, FOX_HD), blk),
            pl.BlockSpec((1, s, FOX_HD), blk),
            pl.BlockSpec((1, s, FOX_HD), blk),
            pl.BlockSpec((1, 1, SUBLANES, s), lambda bi, h, js_ref: (bi, h, 0, 0)),
        ],
        out_specs=pl.BlockSpec((1, s, FOX_HD), blk),
        scratch_shapes=[
            pltpu.VMEM((FOX_GROUP, tq, 1), F32),
            pltpu.VMEM((FOX_GROUP, tq, 1), F32),
            pltpu.VMEM((FOX_GROUP, tq, FOX_HD), F32),
        ],
    )
    return pl.pallas_call(
        functools.partial(_fox_kernel, tq=tq, tk=tk, nq=nq),
        grid_spec=grid_spec,
        out_shape=jax.ShapeDtypeStruct((b, s, FOX_W), BF16),
        compiler_params=_params(("arbitrary", "arbitrary")),
        name="fox",
    )(js, fq, fk, fv, cum)


def _merge_kernel(x_ref, g_ref, wg_ref, bg_ref, yg_ref, yf_ref, mq_ref, mk_ref, mv_ref,
                  wb_ref, wo_ref, o_ref, *, tm, sub):
    d = x_ref.shape[-1]

    def sub_tile(r0, n):
        rows = slice(r0, r0 + n)
        x = x_ref[0, rows, :]
        h = (_rms(x) * g_ref[0]).astype(BF16)

        def gate_logits(i):
            return _dot(h, wg_ref[0, :, i * d:(i + 1) * d]) + bg_ref[0, :, i * d:(i + 1) * d]

        def head(hd):
            return slice(hd * MEM_HD, (hd + 1) * MEM_HD)

        yield
        scores = [_dot_nt(mq_ref[0, rows, head(hd)], mk_ref[0, 0, :, head(hd)])
                  for hd in range(MEM_HEADS)]
        g0, t0 = gate_logits(0), _dot(yg_ref[0, rows, :], wb_ref[0, 0])
        yield
        probs = []
        for s in scores:
            p = jnp.exp(s - jnp.max(s, axis=-1, keepdims=True))
            probs.append((p / jnp.sum(p, axis=-1, keepdims=True)).astype(BF16))
        g1, t1 = gate_logits(1), _dot(yf_ref[0, rows, :], wb_ref[0, 1])
        yield
        merged = _sigmoid(g0) * t0
        y_mem = jnp.concatenate([_dot(p, mv_ref[0, 0, :, head(hd)]) for hd, p in enumerate(probs)],
                                axis=-1).astype(BF16)
        g2 = gate_logits(2)
        yield
        merged = merged + _sigmoid(g1) * t1
        t2 = _dot(y_mem, wb_ref[0, 2])
        yield
        merged = merged + _sigmoid(g2) * t2
        o_ref[0, rows, :] = x + _dot(merged.astype(BF16), wo_ref[0])

    _run_staggered([sub_tile(r0, sub) for r0 in range(0, tm, sub)], lead=2)


def _merge(x, g_mix, w_gate, b_gate, y_gla, y_fox, mq, mk, mv, w_branch, w_out, *, layer, tm, sub):
    b, s, d = x.shape
    m = mk.shape[2]
    tok = lambda i, j: (i, j, 0)
    return pl.pallas_call(
        functools.partial(_merge_kernel, tm=tm, sub=sub),
        grid=(b, s // tm),
        in_specs=[
            pl.BlockSpec((1, tm, d), tok),
            _layer_spec(g_mix, layer),
            _layer_spec(w_gate, layer),
            _layer_spec(b_gate, layer),
            pl.BlockSpec((1, tm, GLA_V), tok),
            pl.BlockSpec((1, tm, FOX_W), tok),
            pl.BlockSpec((1, tm, MEM_W), tok),
            pl.BlockSpec((1, 1, m, MEM_W), lambda i, j: (layer, i, 0, 0)),
            pl.BlockSpec((1, 1, m, MEM_W), lambda i, j: (layer, i, 0, 0)),
            _layer_spec(w_branch, layer),
            _layer_spec(w_out, layer),
        ],
        out_specs=pl.BlockSpec((1, tm, d), tok),
        out_shape=jax.ShapeDtypeStruct((b, s, d), F32),
        compiler_params=_params(("arbitrary", "arbitrary")),
        name="merge",
    )(x, g_mix, w_gate, b_gate, y_gla, y_fox, mq, mk, mv, w_branch, w_out)


def _ffn_kernel(x_ref, g_ref, wg_ref, wu_ref, wd_ref, o_ref, *, tm, sub):
    def sub_tile(r0, n):
        rows = slice(r0, r0 + n)
        x = x_ref[0, rows, :]
        h = (_rms(x) * g_ref[0]).astype(BF16)
        yield
        gate = _dot(h, wg_ref[0])
        up = _dot(h, wu_ref[0])
        yield
        act = (gate * _sigmoid(gate) * up).astype(BF16)
        yield
        o_ref[0, rows, :] = x + _dot(act, wd_ref[0])

    _run_staggered([sub_tile(r0, sub) for r0 in range(0, tm, sub)], lead=2)


def _ffn(x, g_ffn, w_gate, w_up, w_down, *, layer, tm, sub):
    b, s, d = x.shape
    tok = lambda i, j: (i, j, 0)
    return pl.pallas_call(
        functools.partial(_ffn_kernel, tm=tm, sub=sub),
        grid=(b, s // tm),
        in_specs=[
            pl.BlockSpec((1, tm, d), tok),
            _layer_spec(g_ffn, layer),
            _layer_spec(w_gate, layer),
            _layer_spec(w_up, layer),
            _layer_spec(w_down, layer),
        ],
        out_specs=pl.BlockSpec((1, tm, d), tok),
        out_shape=jax.ShapeDtypeStruct((b, s, d), F32),
        compiler_params=_params(("arbitrary", "arbitrary")),
        name="ffn",
    )(x, g_ffn, w_gate, w_up, w_down)


def kernel(x, mem, g_mix, w_in, w_gla_a2, b_gla_a, g_gla_out, b_fox_f, g_fox_q, g_fox_k, g_mem, w_mem_kv, g_mem_q, g_mem_k, b_gate, w_branch, w_out, g_ffn, w_ffn_gate, w_ffn_up, w_ffn_down):
    depth, d = g_mix.shape
    o_ga = 2 * GLA_QK + 2 * GLA_V
    o_fox = o_ga + GLA_RANK
    o_ff = o_fox + 3 * FOX_W
    o_mq = o_ff + FOX_HEADS
    o_bg = o_mq + MEM_W

    mk, mv = _memkv(mem, _rows(g_mem), w_mem_kv.astype(BF16), _rows(g_mem_k))
    w_a, w_g = _wprep(w_in, (o_ga, o_fox, o_ff, o_mq, o_bg), rb=256)
    w2p = jnp.zeros((depth, SMALL_W, GLA_QK), F32).at[:, SMALL_GA:SMALL_GA + GLA_RANK].set(w_gla_a2).astype(BF16)
    b_small = jnp.zeros((depth, 1, SMALL_W), F32).at[:, 0, SMALL_FF:SMALL_FF + FOX_HEADS].set(b_fox_f)
    w_br, w_o = w_branch.astype(BF16), w_out.astype(BF16)
    w_fg, w_fu, w_fd = w_ffn_gate.astype(BF16), w_ffn_up.astype(BF16), w_ffn_down.astype(BF16)
    g_mix3, b_a3, g_go3, g_fq3, g_fk3, g_mq3, b_g3, g_ffn3 = map(
        _rows, (g_mix, b_gla_a, g_gla_out, g_fox_q, g_fox_k, g_mem_q, b_gate, g_ffn))

    for l in range(depth):
        gla_in, fq, fk, fv, mq, cum = _proj(x, g_mix3, w_a, w2p, b_a3, b_small, g_fq3, g_fk3, g_mq3,
                                            layer=l, tm=1024, sub=512)
        y_gla = _gla(gla_in, g_go3, layer=l, t2=512)
        y_fox = _fox(fq, fk, fv, cum, g_fox_q[l], g_fox_k[l], tq=256, tk=256)
        x = _merge(x, g_mix3, w_g, b_g3, y_gla, y_fox, mq, mk, mv, w_br, w_o, layer=l, tm=1024, sub=1024)
        x = _ffn(x, g_ffn3, w_fg, w_fu, w_fd, layer=l, tm=512, sub=256)
    return x
```

```python
import functools

import jax
import jax.numpy as jnp
from jax import lax
from jax.experimental import pallas as pl
from jax.experimental.pallas import tpu as pltpu

EPS = 1e-6
GLA_HEADS = 4
GLA_DK = 64
GLA_DV = 128
GLA_RANK = 16
GLA_TAU = 16.0
GLA_CHUNK = 64
GLA_QK = GLA_HEADS * GLA_DK
GLA_V = GLA_HEADS * GLA_DV
FOX_HEADS = 4
FOX_HD = 128
FOX_W = FOX_HEADS * FOX_HD
MEM_HEADS = 4
MEM_HD = 128
MEM_W = MEM_HEADS * MEM_HD
N_BRANCH = 3
LOG2E = 1.4426950408889634

LANES = 128
SUBLANES = 8
VMEM_LIMIT_BYTES = 60000 * 1024

SMALL_W = LANES
SMALL_FF = 0
SMALL_GA = SUBLANES
GLA_IN_W = 3 * GLA_QK + 2 * GLA_V

BF16 = jnp.bfloat16
F32 = jnp.float32


def _dot(a, b):
    return jnp.dot(a, b, preferred_element_type=F32)


def _dot_nt(a, b):
    return lax.dot_general(a, b, (((1,), (1,)), ((), ())), preferred_element_type=F32)


def _dot_tn(a, b):
    return lax.dot_general(a, b, (((0,), (0,)), ((), ())), preferred_element_type=F32)


def _rms(x):
    return x * lax.rsqrt(jnp.mean(x * x, axis=-1, keepdims=True) + EPS)


def _log_sigmoid(x):
    return jnp.minimum(x, 0.0) - jnp.log1p(jnp.exp(-jnp.abs(x)))


def _sigmoid(x):
    return 1.0 / (1.0 + jnp.exp(-x))


def _head_rms(x, gain, heads, width):
    outs = []
    for h in range(heads):
        outs.append(_rms(x[:, h * width:(h + 1) * width]) * gain)
    return jnp.concatenate(outs, axis=-1)


def _params(sem):
    return pltpu.CompilerParams(dimension_semantics=sem, vmem_limit_bytes=VMEM_LIMIT_BYTES)


def _layer_spec(arr, layer):
    zeros = (0,) * (arr.ndim - 1)
    return pl.BlockSpec((1,) + arr.shape[1:], lambda i, j: (layer,) + zeros, pipeline_mode=pl.Buffered(1))


def _run_staggered(tiles, lead):
    live = {k: t for k, t in enumerate(tiles)}
    step = 0
    while live:
        for k in sorted(live):
            if step >= k * lead and next(live[k], StopIteration) is StopIteration:
                del live[k]
        step += 1


def _rows(p):
    return p.reshape(p.shape[0], 1, p.shape[1])


def _memkv_kernel(mem_ref, g_ref, w_ref, gk_ref, k_ref, v_ref):
    h = (_rms(mem_ref[0]) * g_ref[0]).astype(BF16)
    kv = _dot(h, w_ref[0])
    k_ref[0, 0] = _head_rms(kv[:, :MEM_W], gk_ref[0], MEM_HEADS, MEM_HD).astype(BF16)
    v_ref[0, 0] = kv[:, MEM_W:].astype(BF16)


def _memkv(mem, g_mem, w_kv, g_k):
    b, m, d = mem.shape
    depth = w_kv.shape[0]
    out = jax.ShapeDtypeStruct((depth, b, m, MEM_W), BF16)
    return pl.pallas_call(
        _memkv_kernel,
        grid=(depth, b),
        in_specs=[
            pl.BlockSpec((1, m, d), lambda l, i: (i, 0, 0)),
            pl.BlockSpec((1, 1, d), lambda l, i: (l, 0, 0)),
            pl.BlockSpec((1, d, 2 * MEM_W), lambda l, i: (l, 0, 0)),
            pl.BlockSpec((1, 1, MEM_HD), lambda l, i: (l, 0, 0)),
        ],
        out_specs=[
            pl.BlockSpec((1, 1, m, MEM_W), lambda l, i: (l, i, 0, 0)),
            pl.BlockSpec((1, 1, m, MEM_W), lambda l, i: (l, i, 0, 0)),
        ],
        out_shape=[out, out],
        compiler_params=_params(("arbitrary", "arbitrary")),
        name="memkv",
    )(mem, g_mem, w_kv, g_k)


def _wprep_kernel(w_ref, wa_ref, wg_ref, *, offs):
    o_ga, o_fox, o_ff, o_mq, o_bg, width = offs
    n_main = o_ga
    w = w_ref[0]
    rows = w.shape[0]
    wa_ref[0, :, 0:n_main] = w[:, 0:n_main].astype(BF16)
    wa_ref[0, :, n_main:n_main + 3 * FOX_W] = w[:, o_fox:o_ff].astype(BF16)
    wa_ref[0, :, n_main + 3 * FOX_W:n_main + 3 * FOX_W + MEM_W] = w[:, o_mq:o_bg].astype(BF16)
    small = jnp.concatenate([
        w[:, o_ff:o_mq], jnp.zeros((rows, SMALL_GA - FOX_HEADS), F32),
        w[:, o_ga:o_fox], jnp.zeros((rows, SMALL_W - SMALL_GA - GLA_RANK), F32)], axis=1)
    wa_ref[0, :, n_main + 3 * FOX_W + MEM_W:] = small.astype(BF16)
    wg_ref[0] = w[:, o_bg:width].astype(BF16)


def _wprep(w_in, offs, *, rb):
    depth, d, width = w_in.shape
    o_ga, o_fox, o_ff, o_mq, o_bg = offs
    wa = o_ga + 3 * FOX_W + MEM_W + SMALL_W
    wg = width - o_bg
    return pl.pallas_call(
        functools.partial(_wprep_kernel, offs=offs + (width,)),
        grid=(depth, d // rb),
        in_specs=[pl.BlockSpec((1, rb, width), lambda l, r: (l, r, 0))],
        out_specs=[pl.BlockSpec((1, rb, wa), lambda l, r: (l, r, 0)),
                   pl.BlockSpec((1, rb, wg), lambda l, r: (l, r, 0))],
        out_shape=[jax.ShapeDtypeStruct((depth, d, wa), BF16), jax.ShapeDtypeStruct((depth, d, wg), BF16)],
        compiler_params=_params(("arbitrary", "arbitrary")),
        name="wprep",
    )(w_in)


def _lane_cumsum(x):
    lane = lax.broadcasted_iota(jnp.int32, x.shape, 1)
    shift = 1
    while shift < LANES:
        x = x + jnp.where(lane >= shift, pltpu.roll(x, shift, 1), 0.0)
        shift *= 2
    return x


def _proj_kernel(x_ref, g_ref, w_ref, w2_ref, ba_ref, bs_ref, gfq_ref, gfk_ref, gmq_ref,
                 gla_ref, fq_ref, fk_ref, fv_ref, mq_ref, cum_ref, carry_ref, *, tm, sub):
    @pl.when(pl.program_id(1) == 0)
    def _():
        carry_ref[...] = jnp.zeros_like(carry_ref)

    o_gla, o_fox, o_mq, o_small = 0, 2 * GLA_QK + 2 * GLA_V, 2 * GLA_QK + 2 * GLA_V + 3 * FOX_W, \
        2 * GLA_QK + 2 * GLA_V + 3 * FOX_W + MEM_W

    def sub_tile(r0, n):
        rows = slice(r0, r0 + n)
        h = (_rms(x_ref[0, rows, :]) * g_ref[0]).astype(BF16)
        yield
        small = _dot(h, w_ref[0, :, o_small:o_small + SMALL_W])
        p_qk = _dot(h, w_ref[0, :, o_gla:o_gla + 2 * GLA_QK])
        yield
        a_logit = _dot(small.astype(BF16), w2_ref[0])
        p_vg = _dot(h, w_ref[0, :, o_gla + 2 * GLA_QK:o_fox])
        gla_ref[0, rows, 0:2 * GLA_QK] = p_qk
        gla_ref[0, rows, 2 * GLA_QK:3 * GLA_QK] = _log_sigmoid(a_logit + ba_ref[0]) / GLA_TAU
        yield
        p_fq = _dot(h, w_ref[0, :, o_fox:o_fox + FOX_W])
        gla_ref[0, rows, 3 * GLA_QK:] = p_vg
        yield
        p_fk = _dot(h, w_ref[0, :, o_fox + FOX_W:o_fox + 2 * FOX_W])
        fq_ref[0, rows, :] = (_head_rms(p_fq, gfq_ref[0], FOX_HEADS, FOX_HD)
                              * (FOX_HD ** -0.5 * LOG2E)).astype(BF16)
        yield
        p_fv = _dot(h, w_ref[0, :, o_fox + 2 * FOX_W:o_mq])
        fk_ref[0, rows, :] = _head_rms(p_fk, gfk_ref[0], FOX_HEADS, FOX_HD).astype(BF16)
        yield
        p_mq = _dot(h, w_ref[0, :, o_mq:o_small])
        fv_ref[0, rows, :] = p_fv.astype(BF16)
        yield
        log_f = _log_sigmoid(small + bs_ref[0])
        log_ft = log_f.T[0:SUBLANES, :]
        carry = carry_ref[...]
        blocks = []
        for j in range(n // LANES):
            c = _lane_cumsum(log_ft[:, j * LANES:(j + 1) * LANES]) + carry
            blocks.append(c)
            carry = jnp.broadcast_to(c[:, LANES - 1:LANES], carry.shape)
        carry_ref[...] = carry
        cum = jnp.concatenate(blocks, axis=-1) * LOG2E
        for hd in range(FOX_HEADS):
            cum_ref[0, hd, :, rows] = jnp.broadcast_to(cum[hd:hd + 1, :], (SUBLANES, n))
        yield
        mq_ref[0, rows, :] = (_head_rms(p_mq, gmq_ref[0], MEM_HEADS, MEM_HD) * (MEM_HD ** -0.5)).astype(BF16)

    _run_staggered([sub_tile(r0, sub) for r0 in range(0, tm, sub)], lead=3)


def _proj(x, g_mix, w_a, w2p, b_a, b_small, g_fq, g_fk, g_mq, *, layer, tm, sub):
    b, s, d = x.shape
    tok = lambda i, j: (i, j, 0)
    stacked = (g_mix, w_a, w2p, b_a, b_small, g_fq, g_fk, g_mq)
    return pl.pallas_call(
        functools.partial(_proj_kernel, tm=tm, sub=sub),
        grid=(b, s // tm),
        in_specs=[pl.BlockSpec((1, tm, d), tok)] + [_layer_spec(p, layer) for p in stacked],
        out_specs=[
            pl.BlockSpec((1, tm, GLA_IN_W), tok),
            pl.BlockSpec((1, tm, FOX_W), tok),
            pl.BlockSpec((1, tm, FOX_W), tok),
            pl.BlockSpec((1, tm, FOX_W), tok),
            pl.BlockSpec((1, tm, MEM_W), tok),
            pl.BlockSpec((1, FOX_HEADS, SUBLANES, tm), lambda i, j: (i, 0, 0, j)),
        ],
        out_shape=[
            jax.ShapeDtypeStruct((b, s, GLA_IN_W), F32),
            jax.ShapeDtypeStruct((b, s, FOX_W), BF16),
            jax.ShapeDtypeStruct((b, s, FOX_W), BF16),
            jax.ShapeDtypeStruct((b, s, FOX_W), BF16),
            jax.ShapeDtypeStruct((b, s, MEM_W), BF16),
            jax.ShapeDtypeStruct((b, FOX_HEADS, SUBLANES, s), F32),
        ],
        scratch_shapes=[pltpu.VMEM((SUBLANES, LANES), F32)],
        compiler_params=_params(("arbitrary", "arbitrary")),
        name="proj",
    )(x, *stacked)


def _split3(x):
    hi = x.astype(BF16)
    r = x - hi.astype(F32)
    mid = r.astype(BF16)
    lo = (r - mid.astype(F32)).astype(BF16)
    return hi, mid, lo


GLA_SCAN = 256


def _gla_kernel(in_ref, gain_ref, lt_ref, km_ref, vm_ref, sm_ref, o_ref, st_ref, *, t2):
    @pl.when(pl.program_id(1) == 0)
    def _():
        st_ref[...] = jnp.zeros_like(st_ref)

    c = GLA_CHUNK
    nc = t2 // c
    hc = GLA_HEADS * c
    q = in_ref[0, :, 0:GLA_QK]
    k = in_ref[0, :, GLA_QK:2 * GLA_QK]
    la = in_ref[0, :, 2 * GLA_QK:3 * GLA_QK]
    v = in_ref[0, :, 3 * GLA_QK:3 * GLA_QK + GLA_V]
    gg = in_ref[0, :, 3 * GLA_QK + GLA_V:]

    lt = lt_ref[...]
    parts = []
    for r in range(t2 // GLA_SCAN):
        hi, mid, lo = _split3(la[r * GLA_SCAN:(r + 1) * GLA_SCAN])
        parts.append(_dot(lt, hi) + _dot(lt, mid) + _dot(lt, lo))
    cum = jnp.concatenate(parts, axis=0)
    cum_last = jnp.broadcast_to(cum.reshape(nc, c, GLA_QK)[:, c - 1:c, :], (nc, c, GLA_QK)).reshape(t2, GLA_QK)

    q_in = (q * (GLA_DK ** -0.5) * jnp.exp(cum)).astype(BF16)
    k_in = (k * jnp.exp(-cum)).astype(BF16)
    k_out = (k * jnp.exp(cum_last - cum)).astype(BF16)
    decay = jnp.exp(cum_last)
    vb = v.astype(BF16)
    km = km_ref[...]
    vm = vm_ref[...]
    causal = (lax.broadcasted_iota(jnp.int32, (c, hc), 1) % c
              <= lax.broadcasted_iota(jnp.int32, (c, hc), 0))

    intra, updates = [], []
    for n in range(nc):
        rows = slice(n * c, (n + 1) * c)
        k_bd = jnp.concatenate([k_in[rows]] * GLA_HEADS, axis=0) * km
        v_bd = jnp.concatenate([vb[rows]] * GLA_HEADS, axis=0) * vm
        ko_bd = jnp.concatenate([k_out[rows]] * GLA_HEADS, axis=0) * km
        v_rows = jnp.concatenate([vb[rows, h * GLA_DV:(h + 1) * GLA_DV] for h in range(GLA_HEADS)], axis=0)
        attn = jnp.where(causal, _dot_nt(q_in[rows], k_bd), 0.0).astype(BF16)
        intra.append(_dot(attn, v_bd))
        updates.append(_dot_tn(v_rows, ko_bd))
    st = st_ref[...]
    sm = sm_ref[...]
    outs = []
    for n in range(nc):
        st_bd = jnp.concatenate([st.astype(BF16)] * GLA_HEADS, axis=0) * sm
        outs.append(intra[n] + _dot_nt(q_in[n * c:(n + 1) * c], st_bd))
        st = st * decay[n * c:n * c + 1] + updates[n]
    st_ref[...] = st
    o = jnp.concatenate(outs, axis=0)

    y = jnp.concatenate([_rms(o[:, h * GLA_DV:(h + 1) * GLA_DV]) for h in range(GLA_HEADS)], axis=-1)
    o_ref[0] = (y * gain_ref[0] * (gg * _sigmoid(gg))).astype(BF16)


def _gla(gla_in, gain, *, layer, t2):
    b, s, w = gla_in.shape
    c = GLA_CHUNK
    hc = GLA_HEADS * c
    r = jnp.arange(GLA_SCAN)
    lt = ((r[:, None] // c == r[None, :] // c) & (r[:, None] >= r[None, :])).astype(BF16)
    rh = jnp.arange(hc) // c
    km = (rh[:, None] == jnp.arange(GLA_QK)[None, :] // GLA_DK).astype(BF16)
    vm = (rh[:, None] == jnp.arange(GLA_V)[None, :] // GLA_DV).astype(BF16)
    sm = (jnp.arange(GLA_V)[:, None] // GLA_DV == jnp.arange(GLA_QK)[None, :] // GLA_DK).astype(BF16)
    const = lambda i, j: (0, 0)
    return pl.pallas_call(
        functools.partial(_gla_kernel, t2=t2),
        grid=(b, s // t2),
        in_specs=[
            pl.BlockSpec((1, t2, w), lambda i, j: (i, j, 0)),
            _layer_spec(gain, layer),
            pl.BlockSpec((GLA_SCAN, GLA_SCAN), const),
            pl.BlockSpec((hc, GLA_QK), const),
            pl.BlockSpec((hc, GLA_V), const),
            pl.BlockSpec((GLA_V, GLA_QK), const),
        ],
        out_specs=pl.BlockSpec((1, t2, GLA_V), lambda i, j: (i, j, 0)),
        out_shape=jax.ShapeDtypeStruct((b, s, GLA_V), BF16),
        scratch_shapes=[pltpu.VMEM((GLA_DV, GLA_QK), F32)],
        compiler_params=_params(("arbitrary", "arbitrary")),
        name="gla",
    )(gla_in, gain, lt, km, vm, sm)


FOX_ZERO_EXP = 106.0
FOX_GROUP = 8


def _fox_kernel(js_ref, q_ref, k_ref, v_ref, cum_ref, o_ref, m_ref, l_ref, acc_ref, *, tq, tk, nq):
    bh = pl.program_id(0) * FOX_HEADS + pl.program_id(1)
    win = tk + tq
    rel = (lax.broadcasted_iota(jnp.int32, (tq, win), 1)
           - lax.broadcasted_iota(jnp.int32, (tq, win), 0))

    def load_q(i):
        q0 = pl.multiple_of(i * tq, tq)
        q = q_ref[0, pl.ds(q0, tq), :]
        cq = cum_ref[0, 0, :, pl.ds(q0, tq)].T[:, 0:1]
        return q0, q, cq

    def window_scores(i):
        q0, q, cq = load_q(i)
        k0 = pl.multiple_of(jnp.maximum(q0 - tk, 0), tk)
        s = _dot_nt(q, k_ref[0, pl.ds(k0, win), :]) + cq - cum_ref[0, 0, 0:1, pl.ds(k0, win)]
        return jnp.where(rel <= q0 - k0, s, -jnp.inf), k0

    def window_softmax(slot, s, k0):
        m = jnp.max(s, axis=-1, keepdims=True)
        p = jnp.exp2(s - m)
        m_ref[slot] = m
        l_ref[slot] = jnp.sum(p, axis=-1, keepdims=True)
        acc_ref[slot] = _dot(p.astype(BF16), v_ref[0, pl.ds(k0, win), :])

    def earlier(slot, i):
        first = js_ref[bh * nq + i]
        last = i * (tq // tk) - 1

        @pl.when(first < last)
        def _():
            _, q, cq = load_q(i)

            def body(j, _):
                k0 = pl.multiple_of(j * tk, tk)
                s = _dot_nt(q, k_ref[0, pl.ds(k0, tk), :]) + cq - cum_ref[0, 0, 0:1, pl.ds(k0, tk)]
                m = m_ref[slot]
                m_new = jnp.maximum(m, jnp.max(s, axis=-1, keepdims=True))
                p = jnp.exp2(s - m_new)
                alpha = jnp.exp2(m - m_new)
                l_ref[slot] = alpha * l_ref[slot] + jnp.sum(p, axis=-1, keepdims=True)
                acc_ref[slot] = alpha * acc_ref[slot] + _dot(p.astype(BF16), v_ref[0, pl.ds(k0, tk), :])
                m_ref[slot] = m_new
                return 0

            lax.fori_loop(first, last, body, 0)

    def finish(slot, i):
        q0 = pl.multiple_of(i * tq, tq)
        o_ref[0, pl.ds(q0, tq), :] = (acc_ref[slot] / l_ref[slot]).astype(o_ref.dtype)

    def step(n, _):
        blocks = [n * FOX_GROUP + slot for slot in range(FOX_GROUP)]
        scores = [window_scores(i) for i in blocks]
        for slot, (s, k0) in enumerate(scores):
            window_softmax(slot, s, k0)
        for slot, i in enumerate(blocks):
            earlier(slot, i)
        for slot, i in enumerate(blocks):
            finish(slot, i)
        return 0

    lax.fori_loop(0, nq // FOX_GROUP, step, 0)


def _fox_first_block(cum, g_q, g_k, tq, tk):
    c = cum[:, :, 0, :] / LOG2E
    first = c[:, :, 0::tq]
    last = c[:, :, tk - 1::tk]
    nq, nk = first.shape[-1], last.shape[-1]
    bound = 2.0 * 1.02 * (FOX_HD ** 0.5) * jnp.max(jnp.abs(g_q)) * jnp.max(jnp.abs(g_k))
    needed = first[..., :, None] - last[..., None, :] + bound >= -FOX_ZERO_EXP
    js = jnp.min(jnp.where(needed, jnp.arange(nk, dtype=jnp.int32), nk), axis=-1)
    window_start = jnp.maximum(jnp.arange(nq, dtype=jnp.int32) * (tq // tk) - 1, 0)
    return jnp.minimum(js, window_start).reshape(-1).astype(jnp.int32)


def _fox(fq, fk, fv, cum, g_q, g_k, *, tq, tk):
    b, s, _ = fq.shape
    nq = s // tq
    assert nq % FOX_GROUP == 0 and tq % tk == 0 and s >= tq + tk
    js = _fox_first_block(cum, g_q, g_k, tq, tk)
    blk = lambda bi, h, js_ref: (bi, 0, h)
    grid_spec = pltpu.PrefetchScalarGridSpec(
        num_scalar_prefetch=1,
        grid=(b, FOX_HEADS),
        in_specs=[
            pl.BlockSpec((1, s, FOX_HD), blk),
            pl.BlockSpec((1, s, FOX_HD), blk),
            pl.BlockSpec((1, s, FOX_HD), blk),
            pl.BlockSpec((1, 1, SUBLANES, s), lambda bi, h, js_ref: (bi, h, 0, 0)),
        ],
        out_specs=pl.BlockSpec((1, s, FOX_HD), blk),
        scratch_shapes=[
            pltpu.VMEM((FOX_GROUP, tq, 1), F32),
            pltpu.VMEM((FOX_GROUP, tq, 1), F32),
            pltpu.VMEM((FOX_GROUP, tq, FOX_HD), F32),
        ],
    )
    return pl.pallas_call(
        functools.partial(_fox_kernel, tq=tq, tk=tk, nq=nq),
        grid_spec=grid_spec,
        out_shape=jax.ShapeDtypeStruct((b, s, FOX_W), BF16),
        compiler_params=_params(("arbitrary", "arbitrary")),
        name="fox",
    )(js, fq, fk, fv, cum)


def _merge_kernel(x_ref, g_ref, wg_ref, bg_ref, yg_ref, yf_ref, mq_ref, mk_ref, mv_ref,
                  wb_ref, wo_ref, o_ref, *, tm, sub):
    d = x_ref.shape[-1]

    def sub_tile(r0, n):
        rows = slice(r0, r0 + n)
        x = x_ref[0, rows, :]
        h = (_rms(x) * g_ref[0]).astype(BF16)

        def gate_logits(i):
            return _dot(h, wg_ref[0, :, i * d:(i + 1) * d]) + bg_ref[0, :, i * d:(i + 1) * d]

        def head(hd):
            return slice(hd * MEM_HD, (hd + 1) * MEM_HD)

        yield
        scores = [_dot_nt(mq_ref[0, rows, head(hd)], mk_ref[0, 0, :, head(hd)])
                  for hd in range(MEM_HEADS)]
        g0, t0 = gate_logits(0), _dot(yg_ref[0, rows, :], wb_ref[0, 0])
        yield
        probs = []
        for s in scores:
            p = jnp.exp(s - jnp.max(s, axis=-1, keepdims=True))
            probs.append((p / jnp.sum(p, axis=-1, keepdims=True)).astype(BF16))
        g1, t1 = gate_logits(1), _dot(yf_ref[0, rows, :], wb_ref[0, 1])
        yield
        merged = _sigmoid(g0) * t0
        y_mem = jnp.concatenate([_dot(p, mv_ref[0, 0, :, head(hd)]) for hd, p in enumerate(probs)],
                                axis=-1).astype(BF16)
        g2 = gate_logits(2)
        yield
        merged = merged + _sigmoid(g1) * t1
        t2 = _dot(y_mem, wb_ref[0, 2])
        yield
        merged = merged + _sigmoid(g2) * t2
        o_ref[0, rows, :] = x + _dot(merged.astype(BF16), wo_ref[0])

    _run_staggered([sub_tile(r0, sub) for r0 in range(0, tm, sub)], lead=2)


def _merge(x, g_mix, w_gate, b_gate, y_gla, y_fox, mq, mk, mv, w_branch, w_out, *, layer, tm, sub):
    b, s, d = x.shape
    m = mk.shape[2]
    tok = lambda i, j: (i, j, 0)
    return pl.pallas_call(
        functools.partial(_merge_kernel, tm=tm, sub=sub),
        grid=(b, s // tm),
        in_specs=[
            pl.BlockSpec((1, tm, d), tok),
            _layer_spec(g_mix, layer),
            _layer_spec(w_gate, layer),
            _layer_spec(b_gate, layer),
            pl.BlockSpec((1, tm, GLA_V), tok),
            pl.BlockSpec((1, tm, FOX_W), tok),
            pl.BlockSpec((1, tm, MEM_W), tok),
            pl.BlockSpec((1, 1, m, MEM_W), lambda i, j: (layer, i, 0, 0)),
            pl.BlockSpec((1, 1, m, MEM_W), lambda i, j: (layer, i, 0, 0)),
            _layer_spec(w_branch, layer),
            _layer_spec(w_out, layer),
        ],
        out_specs=pl.BlockSpec((1, tm, d), tok),
        out_shape=jax.ShapeDtypeStruct((b, s, d), F32),
        compiler_params=_params(("arbitrary", "arbitrary")),
        name="merge",
    )(x, g_mix, w_gate, b_gate, y_gla, y_fox, mq, mk, mv, w_branch, w_out)


def _ffn_kernel(x_ref, g_ref, wg_ref, wu_ref, wd_ref, o_ref, *, tm, sub):
    def sub_tile(r0, n):
        rows = slice(r0, r0 + n)
        x = x_ref[0, rows, :]
        h = (_rms(x) * g_ref[0]).astype(BF16)
        yield
        gate = _dot(h, wg_ref[0])
        up = _dot(h, wu_ref[0])
        yield
        act = (gate * _sigmoid(gate) * up).astype(BF16)
        yield
        o_ref[0, rows, :] = x + _dot(act, wd_ref[0])

    _run_staggered([sub_tile(r0, sub) for r0 in range(0, tm, sub)], lead=2)


def _ffn(x, g_ffn, w_gate, w_up, w_down, *, layer, tm, sub):
    b, s, d = x.shape
    tok = lambda i, j: (i, j, 0)
    return pl.pallas_call(
        functools.partial(_ffn_kernel, tm=tm, sub=sub),
        grid=(b, s // tm),
        in_specs=[
            pl.BlockSpec((1, tm, d), tok),
            _layer_spec(g_ffn, layer),
            _layer_spec(w_gate, layer),
            _layer_spec(w_up, layer),
            _layer_spec(w_down, layer),
        ],
        out_specs=pl.BlockSpec((1, tm, d), tok),
        out_shape=jax.ShapeDtypeStruct((b, s, d), F32),
        compiler_params=_params(("arbitrary", "arbitrary")),
        name="ffn",
    )(x, g_ffn, w_gate, w_up, w_down)


def kernel(x, mem, g_mix, w_in, w_gla_a2, b_gla_a, g_gla_out, b_fox_f, g_fox_q, g_fox_k, g_mem, w_mem_kv, g_mem_q, g_mem_k, b_gate, w_branch, w_out, g_ffn, w_ffn_gate, w_ffn_up, w_ffn_down):
    depth, d = g_mix.shape
    o_ga = 2 * GLA_QK + 2 * GLA_V
    o_fox = o_ga + GLA_RANK
    o_ff = o_fox + 3 * FOX_W
    o_mq = o_ff + FOX_HEADS
    o_bg = o_mq + MEM_W

    mk, mv = _memkv(mem, _rows(g_mem), w_mem_kv.astype(BF16), _rows(g_mem_k))
    w_a, w_g = _wprep(w_in, (o_ga, o_fox, o_ff, o_mq, o_bg), rb=256)
    w2p = jnp.zeros((depth, SMALL_W, GLA_QK), F32).at[:, SMALL_GA:SMALL_GA + GLA_RANK].set(w_gla_a2).astype(BF16)
    b_small = jnp.zeros((depth, 1, SMALL_W), F32).at[:, 0, SMALL_FF:SMALL_FF + FOX_HEADS].set(b_fox_f)
    w_br, w_o = w_branch.astype(BF16), w_out.astype(BF16)
    w_fg, w_fu, w_fd = w_ffn_gate.astype(BF16), w_ffn_up.astype(BF16), w_ffn_down.astype(BF16)
    g_mix3, b_a3, g_go3, g_fq3, g_fk3, g_mq3, b_g3, g_ffn3 = map(
        _rows, (g_mix, b_gla_a, g_gla_out, g_fox_q, g_fox_k, g_mem_q, b_gate, g_ffn))

    for l in range(depth):
        gla_in, fq, fk, fv, mq, cum = _proj(x, g_mix3, w_a, w2p, b_a3, b_small, g_fq3, g_fk3, g_mq3,
                                            layer=l, tm=1024, sub=256)
        y_gla = _gla(gla_in, g_go3, layer=l, t2=1024)
        y_fox = _fox(fq, fk, fv, cum, g_fox_q[l], g_fox_k[l], tq=256, tk=256)
        x = _merge(x, g_mix3, w_g, b_g3, y_gla, y_fox, mq, mk, mv, w_br, w_o, layer=l, tm=1024, sub=1024)
        x = _ffn(x, g_ffn3, w_fg, w_fu, w_fd, layer=l, tm=1024, sub=256)
    return x
```

```python
import functools

import jax
import jax.numpy as jnp
from jax import lax
from jax.experimental import pallas as pl
from jax.experimental.pallas import tpu as pltpu

EPS = 1e-6
GLA_HEADS = 4
GLA_DK = 64
GLA_DV = 128
GLA_RANK = 16
GLA_TAU = 16.0
GLA_CHUNK = 64
GLA_QK = GLA_HEADS * GLA_DK
GLA_V = GLA_HEADS * GLA_DV
FOX_HEADS = 4
FOX_HD = 128
FOX_W = FOX_HEADS * FOX_HD
MEM_HEADS = 4
MEM_HD = 128
MEM_W = MEM_HEADS * MEM_HD
N_BRANCH = 3
LOG2E = 1.4426950408889634

LANES = 128
SUBLANES = 8
VMEM_LIMIT_BYTES = 60000 * 1024

SMALL_W = LANES
SMALL_FF = 0
SMALL_GA = SUBLANES

BF16 = jnp.bfloat16
F32 = jnp.float32


def _dot(a, b):
    return jnp.dot(a, b, preferred_element_type=F32)


def _dot_nt(a, b):
    return lax.dot_general(a, b, (((1,), (1,)), ((), ())), preferred_element_type=F32)


def _dot_tn(a, b):
    return lax.dot_general(a, b, (((0,), (0,)), ((), ())), preferred_element_type=F32)


def _rms(x):
    return x * lax.rsqrt(jnp.mean(x * x, axis=-1, keepdims=True) + EPS)


def _log_sigmoid(x):
    return jnp.minimum(x, 0.0) - jnp.log1p(jnp.exp(-jnp.abs(x)))


def _sigmoid(x):
    return 1.0 / (1.0 + jnp.exp(-x))


def _head_rms(x, gain, heads, width):
    outs = []
    for h in range(heads):
        outs.append(_rms(x[:, h * width:(h + 1) * width]) * gain)
    return jnp.concatenate(outs, axis=-1)


def _params(sem):
    return pltpu.CompilerParams(dimension_semantics=sem, vmem_limit_bytes=VMEM_LIMIT_BYTES)


def _layer_spec(arr, layer):
    zeros = (0,) * (arr.ndim - 1)
    return pl.BlockSpec((1,) + arr.shape[1:], lambda i, j: (layer,) + zeros, pipeline_mode=pl.Buffered(1))


def _run_staggered(tiles, lead):
    live = {k: t for k, t in enumerate(tiles)}
    step = 0
    while live:
        for k in sorted(live):
            if step >= k * lead and next(live[k], StopIteration) is StopIteration:
                del live[k]
        step += 1


def _rows(p):
    return p.reshape(p.shape[0], 1, p.shape[1])


def _memkv_kernel(mem_ref, g_ref, w_ref, gk_ref, k_ref, v_ref):
    h = (_rms(mem_ref[0]) * g_ref[0]).astype(BF16)
    kv = _dot(h, w_ref[0])
    k_ref[0, 0] = _head_rms(kv[:, :MEM_W], gk_ref[0], MEM_HEADS, MEM_HD).astype(BF16)
    v_ref[0, 0] = kv[:, MEM_W:].astype(BF16)


def _memkv(mem, g_mem, w_kv, g_k):
    b, m, d = mem.shape
    depth = w_kv.shape[0]
    out = jax.ShapeDtypeStruct((depth, b, m, MEM_W), BF16)
    return pl.pallas_call(
        _memkv_kernel,
        grid=(depth, b),
        in_specs=[
            pl.BlockSpec((1, m, d), lambda l, i: (i, 0, 0)),
            pl.BlockSpec((1, 1, d), lambda l, i: (l, 0, 0)),
            pl.BlockSpec((1, d, 2 * MEM_W), lambda l, i: (l, 0, 0)),
            pl.BlockSpec((1, 1, MEM_HD), lambda l, i: (l, 0, 0)),
        ],
        out_specs=[
            pl.BlockSpec((1, 1, m, MEM_W), lambda l, i: (l, i, 0, 0)),
            pl.BlockSpec((1, 1, m, MEM_W), lambda l, i: (l, i, 0, 0)),
        ],
        out_shape=[out, out],
        compiler_params=_params(("arbitrary", "arbitrary")),
        name="memkv",
    )(mem, g_mem, w_kv, g_k)


def _wprep_kernel(w_ref, wa_ref, wg_ref, *, offs):
    o_ga, o_fox, o_ff, o_mq, o_bg, width = offs
    n_main = o_ga
    w = w_ref[0]
    rows = w.shape[0]
    wa_ref[0, :, 0:n_main] = w[:, 0:n_main].astype(BF16)
    wa_ref[0, :, n_main:n_main + 3 * FOX_W] = w[:, o_fox:o_ff].astype(BF16)
    wa_ref[0, :, n_main + 3 * FOX_W:n_main + 3 * FOX_W + MEM_W] = w[:, o_mq:o_bg].astype(BF16)
    small = jnp.concatenate([
        w[:, o_ff:o_mq], jnp.zeros((rows, SMALL_GA - FOX_HEADS), F32),
        w[:, o_ga:o_fox], jnp.zeros((rows, SMALL_W - SMALL_GA - GLA_RANK), F32)], axis=1)
    wa_ref[0, :, n_main + 3 * FOX_W + MEM_W:] = small.astype(BF16)
    wg_ref[0] = w[:, o_bg:width].astype(BF16)


def _wprep(w_in, offs, *, rb):
    depth, d, width = w_in.shape
    o_ga, o_fox, o_ff, o_mq, o_bg = offs
    wa = o_ga + 3 * FOX_W + MEM_W + SMALL_W
    wg = width - o_bg
    return pl.pallas_call(
        functools.partial(_wprep_kernel, offs=offs + (width,)),
        grid=(depth, d // rb),
        in_specs=[pl.BlockSpec((1, rb, width), lambda l, r: (l, r, 0))],
        out_specs=[pl.BlockSpec((1, rb, wa), lambda l, r: (l, r, 0)),
                   pl.BlockSpec((1, rb, wg), lambda l, r: (l, r, 0))],
        out_shape=[jax.ShapeDtypeStruct((depth, d, wa), BF16), jax.ShapeDtypeStruct((depth, d, wg), BF16)],
        compiler_params=_params(("arbitrary", "arbitrary")),
        name="wprep",
    )(w_in)


def _lane_cumsum(x):
    lane = lax.broadcasted_iota(jnp.int32, x.shape, 1)
    shift = 1
    while shift < LANES:
        x = x + jnp.where(lane >= shift, pltpu.roll(x, shift, 1), 0.0)
        shift *= 2
    return x


def _proj_kernel(x_ref, g_ref, w_ref, w2_ref, ba_ref, bs_ref, gfq_ref, gfk_ref, gmq_ref, ggo_ref,
                 lt_ref, km_ref, vm_ref, sm_ref,
                 yg_ref, fq_ref, fk_ref, fv_ref, mq_ref, cum_ref, carry_ref, st_ref, *, tm, sub):
    @pl.when(pl.program_id(1) == 0)
    def _():
        carry_ref[...] = jnp.zeros_like(carry_ref)
        st_ref[...] = jnp.zeros_like(st_ref)

    o_gla, o_fox, o_mq, o_small = 0, 2 * GLA_QK + 2 * GLA_V, 2 * GLA_QK + 2 * GLA_V + 3 * FOX_W, \
        2 * GLA_QK + 2 * GLA_V + 3 * FOX_W + MEM_W
    gla_consts = (lt_ref[...], km_ref[...], vm_ref[...], sm_ref[...], ggo_ref[0])

    def sub_tile(r0, n):
        rows = slice(r0, r0 + n)
        h = (_rms(x_ref[0, rows, :]) * g_ref[0]).astype(BF16)
        yield
        small = _dot(h, w_ref[0, :, o_small:o_small + SMALL_W])
        p_qk = _dot(h, w_ref[0, :, o_gla:o_gla + 2 * GLA_QK])
        yield
        a_logit = _dot(small.astype(BF16), w2_ref[0])
        p_vg = _dot(h, w_ref[0, :, o_gla + 2 * GLA_QK:o_fox])
        yield
        log_a = _log_sigmoid(a_logit + ba_ref[0]) / GLA_TAU

        def store_gla(y):
            yg_ref[0, rows, :] = y

        gla = _gla_stages(p_qk[:, :GLA_QK], p_qk[:, GLA_QK:], log_a, p_vg[:, :GLA_V], p_vg[:, GLA_V:],
                          gla_consts, st_ref, store_gla)
        p_fq = _dot(h, w_ref[0, :, o_fox:o_fox + FOX_W])
        next(gla)
        yield
        p_fk = _dot(h, w_ref[0, :, o_fox + FOX_W:o_fox + 2 * FOX_W])
        fq_ref[0, rows, :] = (_head_rms(p_fq, gfq_ref[0], FOX_HEADS, FOX_HD)
                              * (FOX_HD ** -0.5 * LOG2E)).astype(BF16)
        next(gla)
        yield
        p_fv = _dot(h, w_ref[0, :, o_fox + 2 * FOX_W:o_mq])
        fk_ref[0, rows, :] = _head_rms(p_fk, gfk_ref[0], FOX_HEADS, FOX_HD).astype(BF16)
        next(gla)
        yield
        p_mq = _dot(h, w_ref[0, :, o_mq:o_small])
        fv_ref[0, rows, :] = p_fv.astype(BF16)
        next(gla)
        yield
        next(gla, None)
        log_f = _log_sigmoid(small + bs_ref[0])
        log_ft = log_f.T[0:SUBLANES, :]
        carry = carry_ref[...]
        blocks = []
        for j in range(n // LANES):
            c = _lane_cumsum(log_ft[:, j * LANES:(j + 1) * LANES]) + carry
            blocks.append(c)
            carry = jnp.broadcast_to(c[:, LANES - 1:LANES], carry.shape)
        carry_ref[...] = carry
        cum = jnp.concatenate(blocks, axis=-1) * LOG2E
        for hd in range(FOX_HEADS):
            cum_ref[0, hd, :, rows] = jnp.broadcast_to(cum[hd:hd + 1, :], (SUBLANES, n))
        yield
        mq_ref[0, rows, :] = (_head_rms(p_mq, gmq_ref[0], MEM_HEADS, MEM_HD) * (MEM_HD ** -0.5)).astype(BF16)

    _run_staggered([sub_tile(r0, sub) for r0 in range(0, tm, sub)], lead=3)


def _proj(x, g_mix, w_a, w2p, b_a, b_small, g_fq, g_fk, g_mq, g_go, *, layer, tm, sub):
    b, s, d = x.shape
    assert sub % GLA_SCAN == 0
    tok = lambda i, j: (i, j, 0)
    stacked = (g_mix, w_a, w2p, b_a, b_small, g_fq, g_fk, g_mq, g_go)
    consts = _gla_constants()
    return pl.pallas_call(
        functools.partial(_proj_kernel, tm=tm, sub=sub),
        grid=(b, s // tm),
        in_specs=([pl.BlockSpec((1, tm, d), tok)] + [_layer_spec(p, layer) for p in stacked]
                  + [pl.BlockSpec(c.shape, lambda i, j: (0, 0), pipeline_mode=pl.Buffered(1)) for c in consts]),
        out_specs=[
            pl.BlockSpec((1, tm, GLA_V), tok),
            pl.BlockSpec((1, tm, FOX_W), tok),
            pl.BlockSpec((1, tm, FOX_W), tok),
            pl.BlockSpec((1, tm, FOX_W), tok),
            pl.BlockSpec((1, tm, MEM_W), tok),
            pl.BlockSpec((1, FOX_HEADS, SUBLANES, tm), lambda i, j: (i, 0, 0, j)),
        ],
        out_shape=[
            jax.ShapeDtypeStruct((b, s, GLA_V), BF16),
            jax.ShapeDtypeStruct((b, s, FOX_W), BF16),
            jax.ShapeDtypeStruct((b, s, FOX_W), BF16),
            jax.ShapeDtypeStruct((b, s, FOX_W), BF16),
            jax.ShapeDtypeStruct((b, s, MEM_W), BF16),
            jax.ShapeDtypeStruct((b, FOX_HEADS, SUBLANES, s), F32),
        ],
        scratch_shapes=[pltpu.VMEM((SUBLANES, LANES), F32), pltpu.VMEM((GLA_DV, GLA_QK), F32)],
        compiler_params=_params(("arbitrary", "arbitrary")),
        name="proj",
    )(x, *stacked, *consts)


def _split3(x):
    hi = x.astype(BF16)
    r = x - hi.astype(F32)
    mid = r.astype(BF16)
    lo = (r - mid.astype(F32)).astype(BF16)
    return hi, mid, lo


GLA_SCAN = 256


def _gla_stages(q, k, la, v, gg, consts, st_ref, store):
    lt, km, vm, sm, gain = consts
    n = q.shape[0]
    c = GLA_CHUNK
    nc = n // c
    hc = GLA_HEADS * c

    parts = []
    for r in range(n // GLA_SCAN):
        hi, mid, lo = _split3(la[r * GLA_SCAN:(r + 1) * GLA_SCAN])
        parts.append(_dot(lt, hi) + _dot(lt, mid) + _dot(lt, lo))
    yield
    cum = jnp.concatenate(parts, axis=0)
    cum_last = jnp.broadcast_to(cum.reshape(nc, c, GLA_QK)[:, c - 1:c, :], (nc, c, GLA_QK)).reshape(n, GLA_QK)
    q_in = (q * (GLA_DK ** -0.5) * jnp.exp(cum)).astype(BF16)
    k_in = (k * jnp.exp(-cum)).astype(BF16)
    k_out = (k * jnp.exp(cum_last - cum)).astype(BF16)
    decay = jnp.exp(cum_last)
    vb = v.astype(BF16)
    causal = (lax.broadcasted_iota(jnp.int32, (c, hc), 1) % c
              <= lax.broadcasted_iota(jnp.int32, (c, hc), 0))
    yield
    intra, updates = [], []
    for i in range(nc):
        rows = slice(i * c, (i + 1) * c)
        k_bd =jnp.concatenate([k_in[rows]] * GLA_HEADS, axis=0) * km
        v_bd = jnp.concatenate([vb[rows]] * GLA_HEADS, axis=0) * vm
        ko_bd = jnp.concatenate([k_out[rows]] * GLA_HEADS, axis=0) * km
        v_rows = jnp.concatenate([vb[rows, h * GLA_DV:(h + 1) * GLA_DV] for h in range(GLA_HEADS)], axis=0)
        attn = jnp.where(causal, _dot_nt(q_in[rows], k_bd), 0.0).astype(BF16)
        intra.append(_dot(attn, v_bd))
        updates.append(_dot_tn(v_rows, ko_bd))
    yield
    st = st_ref[...]
    outs = []
    for i in range(nc):
        st_bd = jnp.concatenate([st.astype(BF16)] * GLA_HEADS, axis=0) * sm
        outs.append(intra[i] + _dot_nt(q_in[i * c:(i + 1) * c], st_bd))
        st = st * decay[i * c:i * c + 1] + updates[i]
    st_ref[...] = st
    yield
    o = jnp.concatenate(outs, axis=0)
    y = jnp.concatenate([_rms(o[:, h * GLA_DV:(h + 1) * GLA_DV]) for h in range(GLA_HEADS)], axis=-1)
    store((y * gain * (gg * _sigmoid(gg))).astype(BF16))


def _gla_constants():
    c = GLA_CHUNK
    r = jnp.arange(GLA_SCAN)
    lt = ((r[:, None] // c == r[None, :] // c) & (r[:, None] >= r[None, :])).astype(BF16)
    rh = jnp.arange(GLA_HEADS * c) // c
    km = (rh[:, None] == jnp.arange(GLA_QK)[None, :] // GLA_DK).astype(BF16)
    vm = (rh[:, None] == jnp.arange(GLA_V)[None, :] // GLA_DV).astype(BF16)
    sm = (jnp.arange(GLA_V)[:, None] // GLA_DV == jnp.arange(GLA_QK)[None, :] // GLA_DK).astype(BF16)
    return lt, km, vm, sm


FOX_ZERO_EXP = 106.0
FOX_GROUP = 8


def _fox_kernel(js_ref, q_ref, k_ref, v_ref, cum_ref, mask_ref, o_ref, va_ref, m_ref, acc_ref, *, tq, tk, nq):
    bh = pl.program_id(0) * FOX_HEADS + pl.program_id(1)
    win = tk + tq
    lane = lax.broadcasted_iota(jnp.int32, (v_ref.shape[1], FOX_HD), 1)
    va_ref[:, 0:FOX_HD] = v_ref[0]
    va_ref[:, FOX_HD:] = jnp.where(lane == 0, 1.0, 0.0).astype(BF16)

    def load_q(i):
        q0 = pl.multiple_of(i * tq, tq)
        q = q_ref[0, pl.ds(q0, tq), :]
        origin = cum_ref[0, 0, 0:1, pl.ds(q0, tq)][:, 0:1]
        return q0, q, origin

    def window_scores(i):
        q0, q, origin = load_q(i)
        k0 = pl.multiple_of(jnp.maximum(q0 - tk, 0), tk)
        s = _dot_nt(q, k_ref[0, pl.ds(k0, win), :]) + (origin - cum_ref[0, 0, 0:1, pl.ds(k0, win)])
        return s + mask_ref[jnp.minimum(i, 1)], k0

    def window_softmax(slot, s, k0):
        m = jnp.max(s, axis=-1, keepdims=True)
        m_ref[slot] = m
        acc_ref[slot] = _dot(jnp.exp2(s - m).astype(BF16), va_ref[pl.ds(k0, win), :])

    def earlier(slot, i):
        first = js_ref[bh * nq + i]
        last = i * (tq // tk) - 1

        @pl.when(first < last)
        def _():
            _, q, origin = load_q(i)

            def body(j, _):
                k0 = pl.multiple_of(j * tk, tk)
                s = _dot_nt(q, k_ref[0, pl.ds(k0, tk), :]) + (origin - cum_ref[0, 0, 0:1, pl.ds(k0, tk)])
                m = m_ref[slot]
                m_new = jnp.maximum(m, jnp.max(s, axis=-1, keepdims=True))
                p = jnp.exp2(s - m_new).astype(BF16)
                acc_ref[slot] = jnp.exp2(m - m_new) * acc_ref[slot] + _dot(p, va_ref[pl.ds(k0, tk), :])
                m_ref[slot] = m_new
                return 0

            lax.fori_loop(first, last, body, 0)

    def finish(slot, i):
        q0 = pl.multiple_of(i * tq, tq)
        acc = acc_ref[slot]
        o_ref[0, pl.ds(q0, tq), :] = (acc[:, :FOX_HD] / acc[:, FOX_HD:FOX_HD + 1]).astype(o_ref.dtype)

    def step(n, _):
        blocks = [n * FOX_GROUP + slot for slot in range(FOX_GROUP)]
        scores = [window_scores(i) for i in blocks]
        for slot, (s, k0) in enumerate(scores):
            window_softmax(slot, s, k0)
        for slot, i in enumerate(blocks):
            earlier(slot, i)
        for slot, i in enumerate(blocks):
            finish(slot, i)
        return 0

    lax.fori_loop(0, nq // FOX_GROUP, step, 0)


def _fox_first_block(cum, g_q, g_k, tq, tk):
    c = cum[:, :, 0, :] / LOG2E
    first = c[:, :, 0::tq]
    last = c[:, :, tk - 1::tk]
    nq, nk = first.shape[-1], last.shape[-1]
    bound = 2.0 * 1.02 * (FOX_HD ** 0.5) * jnp.max(jnp.abs(g_q)) * jnp.max(jnp.abs(g_k))
    needed = first[..., :, None] - last[..., None, :] + bound >= -FOX_ZERO_EXP
    js = jnp.min(jnp.where(needed, jnp.arange(nk, dtype=jnp.int32), nk), axis=-1)
    window_start = jnp.maximum(jnp.arange(nq, dtype=jnp.int32) * (tq // tk) - 1, 0)
    return jnp.minimum(js, window_start).reshape(-1).astype(jnp.int32)


def _fox(fq, fk, fv, cum, g_q, g_k, *, tq, tk):
    b, s, _ = fq.shape
    nq = s // tq
    assert nq % FOX_GROUP == 0 and tq % tk == 0 and s >= tq + tk
    js = _fox_first_block(cum, g_q, g_k, tq, tk)
    rel = jnp.arange(tk + tq)[None, :] - jnp.arange(tq)[:, None]
    mask = jnp.stack([jnp.where(rel <= 0, 0.0, -jnp.inf), jnp.where(rel <= tk, 0.0, -jnp.inf)]).astype(F32)
    blk = lambda bi, h, js_ref: (bi, 0, h)
    grid_spec = pltpu.PrefetchScalarGridSpec(
        num_scalar_prefetch=1,
        grid=(b, FOX_HEADS),
        in_specs=[
            pl.BlockSpec((1, s, FOX_HD), blk),
            pl.BlockSpec((1, s, FOX_HD), blk),
            pl.BlockSpec((1, s, FOX_HD), blk),
            pl.BlockSpec((1, 1, SUBLANES, s), lambda bi, h, js_ref: (bi, h, 0, 0)),
            pl.BlockSpec((2, tq, tk + tq), lambda bi, h, js_ref: (0, 0, 0)),
        ],
        out_specs=pl.BlockSpec((1, s, FOX_HD), blk),
        scratch_shapes=[
            pltpu.VMEM((s, 2 * FOX_HD), BF16),
            pltpu.VMEM((FOX_GROUP, tq, 1), F32),
            pltpu.VMEM((FOX_GROUP, tq, 2 * FOX_HD), F32),
        ],
    )
    return pl.pallas_call(
        functools.partial(_fox_kernel, tq=tq, tk=tk, nq=nq),
        grid_spec=grid_spec,
        out_shape=jax.ShapeDtypeStruct((b, s, FOX_W), BF16),
        compiler_params=_params(("arbitrary", "arbitrary")),
        name="fox",
    )(js, fq, fk, fv, cum, mask)


def _merge_kernel(x_ref, g_ref, wg_ref, bg_ref, yg_ref, yf_ref, mq_ref, mk_ref, mv_ref,
                  wb_ref, wo_ref, o_ref, *, tm, sub):
    d = x_ref.shape[-1]

    def sub_tile(r0, n):
        rows = slice(r0, r0 + n)
        x = x_ref[0, rows, :]
        h = (_rms(x) * g_ref[0]).astype(BF16)

        def gate_logits(i):
            return _dot(h, wg_ref[0, :, i * d:(i + 1) * d]) + bg_ref[0, :, i * d:(i + 1) * d]

        def head(hd):
            return slice(hd * MEM_HD, (hd + 1) * MEM_HD)

        yield
        scores = [_dot_nt(mq_ref[0, rows, head(hd)], mk_ref[0, 0, :, head(hd)])
                  for hd in range(MEM_HEADS)]
        g0, t0 = gate_logits(0), _dot(yg_ref[0, rows, :], wb_ref[0, 0])
        yield
        probs = []
        for s in scores:
            p = jnp.exp(s - jnp.max(s, axis=-1, keepdims=True))
            probs.append((p / jnp.sum(p, axis=-1, keepdims=True)).astype(BF16))
        g1, t1 = gate_logits(1), _dot(yf_ref[0, rows, :], wb_ref[0, 1])
        yield
        merged = _sigmoid(g0) * t0
        y_mem = jnp.concatenate([_dot(p, mv_ref[0, 0, :, head(hd)]) for hd, p in enumerate(probs)],
                                axis=-1).astype(BF16)
        g2 = gate_logits(2)
        yield
        merged = merged + _sigmoid(g1) * t1
        t2 = _dot(y_mem, wb_ref[0, 2])
        yield
        merged = merged + _sigmoid(g2) * t2
        o_ref[0, rows, :] = x + _dot(merged.astype(BF16), wo_ref[0])

    _run_staggered([sub_tile(r0, sub) for r0 in range(0, tm, sub)], lead=2)


def _merge(x, g_mix, w_gate, b_gate, y_gla, y_fox, mq, mk, mv, w_branch, w_out, *, layer, tm, sub):
    b, s, d = x.shape
    m = mk.shape[2]
    tok = lambda i, j: (i, j, 0)
    return pl.pallas_call(
        functools.partial(_merge_kernel, tm=tm, sub=sub),
        grid=(b, s // tm),
        in_specs=[
            pl.BlockSpec((1, tm, d), tok),
            _layer_spec(g_mix, layer),
            _layer_spec(w_gate, layer),
            _layer_spec(b_gate, layer),
            pl.BlockSpec((1, tm, GLA_V), tok),
            pl.BlockSpec((1, tm, FOX_W), tok),
            pl.BlockSpec((1, tm, MEM_W), tok),
            pl.BlockSpec((1, 1, m, MEM_W), lambda i, j: (layer, i, 0, 0)),
            pl.BlockSpec((1, 1, m, MEM_W), lambda i, j: (layer, i, 0, 0)),
            _layer_spec(w_branch, layer),
            _layer_spec(w_out, layer),
        ],
        out_specs=pl.BlockSpec((1, tm, d), tok),
        out_shape=jax.ShapeDtypeStruct((b, s, d), F32),
        compiler_params=_params(("arbitrary", "arbitrary")),
        name="merge",
    )(x, g_mix, w_gate, b_gate, y_gla, y_fox, mq, mk, mv, w_branch, w_out)


def _ffn_kernel(x_ref, g_ref, wg_ref, wu_ref, wd_ref, o_ref, *, tm, sub):
    def sub_tile(r0, n):
        rows = slice(r0, r0 + n)
        x = x_ref[0, rows, :]
        h = (_rms(x) * g_ref[0]).astype(BF16)
        yield
        gate = _dot(h, wg_ref[0])
        up = _dot(h, wu_ref[0])
        yield
        act = (gate * _sigmoid(gate) * up).astype(BF16)
        yield
        o_ref[0, rows, :] = x + _dot(act, wd_ref[0])

    _run_staggered([sub_tile(r0, sub) for r0 in range(0, tm, sub)], lead=2)


def _ffn(x, g_ffn, w_gate, w_up, w_down, *, layer, tm, sub):
    b, s, d = x.shape
    tok = lambda i, j: (i, j, 0)
    return pl.pallas_call(
        functools.partial(_ffn_kernel, tm=tm, sub=sub),
        grid=(b, s // tm),
        in_specs=[
            pl.BlockSpec((1, tm, d), tok),
            _layer_spec(g_ffn, layer),
            _layer_spec(w_gate, layer),
            _layer_spec(w_up, layer),
            _layer_spec(w_down, layer),
        ],
        out_specs=pl.BlockSpec((1, tm, d), tok),
        out_shape=jax.ShapeDtypeStruct((b, s, d), F32),
        compiler_params=_params(("arbitrary", "arbitrary")),
        name="ffn",
    )(x, g_ffn, w_gate, w_up, w_down)


def kernel(x, mem, g_mix, w_in, w_gla_a2, b_gla_a, g_gla_out, b_fox_f, g_fox_q, g_fox_k, g_mem, w_mem_kv, g_mem_q, g_mem_k, b_gate, w_branch, w_out, g_ffn, w_ffn_gate, w_ffn_up, w_ffn_down):
    depth, d = g_mix.shape
    o_ga = 2 * GLA_QK + 2 * GLA_V
    o_fox = o_ga + GLA_RANK
    o_ff = o_fox + 3 * FOX_W
    o_mq = o_ff + FOX_HEADS
    o_bg = o_mq + MEM_W

    mk, mv = _memkv(mem, _rows(g_mem), w_mem_kv.astype(BF16), _rows(g_mem_k))
    w_a, w_g = _wprep(w_in, (o_ga, o_fox, o_ff, o_mq, o_bg), rb=256)
    w2p = jnp.zeros((depth, SMALL_W, GLA_QK), F32).at[:, SMALL_GA:SMALL_GA + GLA_RANK].set(w_gla_a2).astype(BF16)
    b_small = jnp.zeros((depth, 1, SMALL_W), F32).at[:, 0, SMALL_FF:SMALL_FF + FOX_HEADS].set(b_fox_f)
    w_br, w_o = w_branch.astype(BF16), w_out.astype(BF16)
    w_fg, w_fu, w_fd = w_ffn_gate.astype(BF16), w_ffn_up.astype(BF16), w_ffn_down.astype(BF16)
    g_mix3, b_a3, g_go3, g_fq3, g_fk3, g_mq3, b_g3, g_ffn3 = map(
        _rows, (g_mix, b_gla_a, g_gla_out, g_fox_q, g_fox_k, g_mem_q, b_gate, g_ffn))

    for l in range(depth):
        y_gla, fq, fk, fv, mq, cum = _proj(x, g_mix3, w_a, w2p, b_a3, b_small, g_fq3, g_fk3, g_mq3, g_go3,
                                           layer=l, tm=1024, sub=256)
        y_fox = _fox(fq, fk, fv, cum, g_fox_q[l], g_fox_k[l], tq=256, tk=256)
        x = _merge(x, g_mix3, w_g, b_g3, y_gla, y_fox, mq, mk, mv, w_br, w_o, layer=l, tm=1024, sub=1024)
        x = _ffn(x, g_ffn3, w_fg, w_fu, w_fd, layer=l, tm=1024, sub=256)
    return x
```

```python
import functools

import jax
import jax.numpy as jnp
from jax import lax
from jax.experimental import pallas as pl
from jax.experimental.pallas import tpu as pltpu

EPS = 1e-6
GLA_HEADS = 4
GLA_DK = 64
GLA_DV = 128
GLA_RANK = 16
GLA_TAU = 16.0
GLA_CHUNK = 64
GLA_QK = GLA_HEADS * GLA_DK
GLA_V = GLA_HEADS * GLA_DV
FOX_HEADS = 4
FOX_HD = 128
FOX_W = FOX_HEADS * FOX_HD
MEM_HEADS = 4
MEM_HD = 128
MEM_W = MEM_HEADS * MEM_HD
N_BRANCH = 3
LOG2E = 1.4426950408889634

LANES = 128
SUBLANES = 8
VMEM_LIMIT_BYTES = 60000 * 1024

TOKEN_TILE = 1024
PROJ_SUB = 512
MERGE_SUB = 1024
FFN_SUB = 256
FOX_TQ = 256
FOX_TK = 256
WPREP_ROWS = 256

SMALL_W = LANES
SMALL_FF = 0
SMALL_GA = SUBLANES

BF16 = jnp.bfloat16
F32 = jnp.float32


def _dot(a, b):
    return jnp.dot(a, b, preferred_element_type=F32)


def _dot_nt(a, b):
    return lax.dot_general(a, b, (((1,), (1,)), ((), ())), preferred_element_type=F32)


def _dot_tn(a, b):
    return lax.dot_general(a, b, (((0,), (0,)), ((), ())), preferred_element_type=F32)


def _rms(x):
    return x * lax.rsqrt(jnp.mean(x * x, axis=-1, keepdims=True) + EPS)


def _log_sigmoid(x):
    return jnp.minimum(x, 0.0) - jnp.log1p(jnp.exp(-jnp.abs(x)))


def _sigmoid(x):
    return 1.0 / (1.0 + jnp.exp(-x))


def _head_rms(x, gain, heads, width):
    outs = []
    for h in range(heads):
        outs.append(_rms(x[:, h * width:(h + 1) * width]) * gain)
    return jnp.concatenate(outs, axis=-1)


def _params(sem):
    return pltpu.CompilerParams(dimension_semantics=sem, vmem_limit_bytes=VMEM_LIMIT_BYTES)


def _layer_spec(arr, layer):
    zeros = (0,) * (arr.ndim - 1)
    return pl.BlockSpec((1,) + arr.shape[1:], lambda i, j: (layer,) + zeros, pipeline_mode=pl.Buffered(1))


def _run_staggered(tiles, lead):
    live = {k: t for k, t in enumerate(tiles)}
    step = 0
    while live:
        for k in sorted(live):
            if step >= k * lead and next(live[k], StopIteration) is StopIteration:
                del live[k]
        step += 1


def _rows(p):
    return p.reshape(p.shape[0], 1, p.shape[1])


def _memkv_kernel(mem_ref, g_ref, w_ref, gk_ref, k_ref, v_ref):
    h = (_rms(mem_ref[0]) * g_ref[0]).astype(BF16)
    kv = _dot(h, w_ref[0])
    k_ref[0, 0] = _head_rms(kv[:, :MEM_W], gk_ref[0], MEM_HEADS, MEM_HD).astype(BF16)
    v_ref[0, 0] = kv[:, MEM_W:].astype(BF16)


def _memkv(mem, g_mem, w_kv, g_k):
    b, m, d = mem.shape
    depth = w_kv.shape[0]
    out = jax.ShapeDtypeStruct((depth, b, m, MEM_W), BF16)
    return pl.pallas_call(
        _memkv_kernel,
        grid=(depth, b),
        in_specs=[
            pl.BlockSpec((1, m, d), lambda l, i: (i, 0, 0)),
            pl.BlockSpec((1, 1, d), lambda l, i: (l, 0, 0)),
            pl.BlockSpec((1, d, 2 * MEM_W), lambda l, i: (l, 0, 0)),
            pl.BlockSpec((1, 1, MEM_HD), lambda l, i: (l, 0, 0)),
        ],
        out_specs=[
            pl.BlockSpec((1, 1, m, MEM_W), lambda l, i: (l, i, 0, 0)),
            pl.BlockSpec((1, 1, m, MEM_W), lambda l, i: (l, i, 0, 0)),
        ],
        out_shape=[out, out],
        compiler_params=_params(("arbitrary", "arbitrary")),
        name="memkv",
    )(mem, g_mem, w_kv, g_k)


def _wprep_kernel(w_ref, wa_ref, wg_ref, *, offs):
    o_ga, o_fox, o_ff, o_mq, o_bg, width = offs
    n_main = o_ga
    w = w_ref[0]
    rows = w.shape[0]
    wa_ref[0, :, 0:n_main] = w[:, 0:n_main].astype(BF16)
    wa_ref[0, :, n_main:n_main + 3 * FOX_W] = w[:, o_fox:o_ff].astype(BF16)
    wa_ref[0, :, n_main + 3 * FOX_W:n_main + 3 * FOX_W + MEM_W] = w[:, o_mq:o_bg].astype(BF16)
    small = jnp.concatenate([
        w[:, o_ff:o_mq], jnp.zeros((rows, SMALL_GA - FOX_HEADS), F32),
        w[:, o_ga:o_fox], jnp.zeros((rows, SMALL_W - SMALL_GA - GLA_RANK), F32)], axis=1)
    wa_ref[0, :, n_main + 3 * FOX_W + MEM_W:] = small.astype(BF16)
    wg_ref[0] = w[:, o_bg:width].astype(BF16)


def _wprep(w_in, offs, *, rb):
    depth, d, width = w_in.shape
    o_ga, o_fox, o_ff, o_mq, o_bg = offs
    wa = o_ga + 3 * FOX_W + MEM_W + SMALL_W
    wg = width - o_bg
    return pl.pallas_call(
        functools.partial(_wprep_kernel, offs=offs + (width,)),
        grid=(depth, d // rb),
        in_specs=[pl.BlockSpec((1, rb, width), lambda l, r: (l, r, 0))],
        out_specs=[pl.BlockSpec((1, rb, wa), lambda l, r: (l, r, 0)),
                   pl.BlockSpec((1, rb, wg), lambda l, r: (l, r, 0))],
        out_shape=[jax.ShapeDtypeStruct((depth, d, wa), BF16), jax.ShapeDtypeStruct((depth, d, wg), BF16)],
        compiler_params=_params(("arbitrary", "arbitrary")),
        name="wprep",
    )(w_in)


def _lane_cumsum(x):
    lane = lax.broadcasted_iota(jnp.int32, x.shape, 1)
    shift = 1
    while shift < LANES:
        x = x + jnp.where(lane >= shift, pltpu.roll(x, shift, 1), 0.0)
        shift *= 2
    return x


def _proj_kernel(x_ref, g_ref, w_ref, w2_ref, ba_ref, bs_ref, gfq_ref, gfk_ref, gmq_ref, ggo_ref,
                 lt_ref, km_ref, vm_ref, sm_ref,
                 yg_ref, fq_ref, fk_ref, fv_ref, mq_ref, cum_ref, carry_ref, st_ref, *, tm, sub):
    @pl.when(pl.program_id(1) == 0)
    def _():
        carry_ref[...] = jnp.zeros_like(carry_ref)
        st_ref[...] = jnp.zeros_like(st_ref)

    o_gla, o_fox, o_mq, o_small = 0, 2 * GLA_QK + 2 * GLA_V, 2 * GLA_QK + 2 * GLA_V + 3 * FOX_W, \
        2 * GLA_QK + 2 * GLA_V + 3 * FOX_W + MEM_W
    gla_consts = (lt_ref[...], km_ref[...], vm_ref[...], sm_ref[...], ggo_ref[0])

    def sub_tile(r0, n):
        rows = slice(r0, r0 + n)
        h = (_rms(x_ref[0, rows, :]) * g_ref[0]).astype(BF16)
        yield
        small = _dot(h, w_ref[0, :, o_small:o_small + SMALL_W])
        p_qk = _dot(h, w_ref[0, :, o_gla:o_gla + 2 * GLA_QK])
        yield
        a_logit = _dot(small.astype(BF16), w2_ref[0])
        p_vg = _dot(h, w_ref[0, :, o_gla + 2 * GLA_QK:o_fox])
        yield
        log_a = _log_sigmoid(a_logit + ba_ref[0]) / GLA_TAU

        def store_gla(y):
            yg_ref[0, rows, :] = y

        gla = _gla_stages(p_qk[:, :GLA_QK], p_qk[:, GLA_QK:], log_a, p_vg[:, :GLA_V], p_vg[:, GLA_V:],
                          gla_consts, st_ref, store_gla)
        p_fq = _dot(h, w_ref[0, :, o_fox:o_fox + FOX_W])
        next(gla)
        yield
        p_fk = _dot(h, w_ref[0, :, o_fox + FOX_W:o_fox + 2 * FOX_W])
        fq_ref[0, rows, :] = (_head_rms(p_fq, gfq_ref[0], FOX_HEADS, FOX_HD)
                              * (FOX_HD ** -0.5 * LOG2E)).astype(BF16)
        next(gla)
        yield
        p_fv = _dot(h, w_ref[0, :, o_fox + 2 * FOX_W:o_mq])
        fk_ref[0, rows, :] = _head_rms(p_fk, gfk_ref[0], FOX_HEADS, FOX_HD).astype(BF16)
        next(gla)
        yield
        p_mq = _dot(h, w_ref[0, :, o_mq:o_small])
        fv_ref[0, rows, :] = p_fv.astype(BF16)
        next(gla)
        yield
        next(gla, None)
        log_f = _log_sigmoid(small + bs_ref[0])
        log_ft = log_f.T[0:SUBLANES, :]
        carry = carry_ref[...]
        blocks = []
        for j in range(n // LANES):
            c = _lane_cumsum(log_ft[:, j * LANES:(j + 1) * LANES]) + carry
            blocks.append(c)
            carry = jnp.broadcast_to(c[:, LANES - 1:LANES], carry.shape)
        carry_ref[...] = carry
        cum = jnp.concatenate(blocks, axis=-1) * LOG2E
        for hd in range(FOX_HEADS):
            cum_ref[0, hd, :, rows] = jnp.broadcast_to(cum[hd:hd + 1, :], (SUBLANES, n))
        yield
        mq_ref[0, rows, :] = (_head_rms(p_mq, gmq_ref[0], MEM_HEADS, MEM_HD) * (MEM_HD ** -0.5)).astype(BF16)

    _run_staggered([sub_tile(r0, sub) for r0 in range(0, tm, sub)], lead=3)


def _proj(x, g_mix, w_a, w2p, b_a, b_small, g_fq, g_fk, g_mq, g_go, *, layer, tm, sub):
    b, s, d = x.shape
    assert sub % GLA_SCAN == 0
    tok = lambda i, j: (i, j, 0)
    stacked = (g_mix, w_a, w2p, b_a, b_small, g_fq, g_fk, g_mq, g_go)
    consts = _gla_constants()
    return pl.pallas_call(
        functools.partial(_proj_kernel, tm=tm, sub=sub),
        grid=(b, s // tm),
        in_specs=([pl.BlockSpec((1, tm, d), tok)] + [_layer_spec(p, layer) for p in stacked]
                  + [pl.BlockSpec(c.shape, lambda i, j: (0, 0), pipeline_mode=pl.Buffered(1)) for c in consts]),
        out_specs=[
            pl.BlockSpec((1, tm, GLA_V), tok),
            pl.BlockSpec((1, tm, FOX_W), tok),
            pl.BlockSpec((1, tm, FOX_W), tok),
            pl.BlockSpec((1, tm, FOX_W), tok),
            pl.BlockSpec((1, tm, MEM_W), tok),
            pl.BlockSpec((1, FOX_HEADS, SUBLANES, tm), lambda i, j: (i, 0, 0, j)),
        ],
        out_shape=[
            jax.ShapeDtypeStruct((b, s, GLA_V), BF16),
            jax.ShapeDtypeStruct((b, s, FOX_W), BF16),
            jax.ShapeDtypeStruct((b, s, FOX_W), BF16),
            jax.ShapeDtypeStruct((b, s, FOX_W), BF16),
            jax.ShapeDtypeStruct((b, s, MEM_W), BF16),
            jax.ShapeDtypeStruct((b, FOX_HEADS, SUBLANES, s), F32),
        ],
        scratch_shapes=[pltpu.VMEM((SUBLANES, LANES), F32), pltpu.VMEM((GLA_DV, GLA_QK), F32)],
        compiler_params=_params(("arbitrary", "arbitrary")),
        name="proj",
    )(x, *stacked, *consts)


def _split3(x):
    hi = x.astype(BF16)
    r = x - hi.astype(F32)
    mid = r.astype(BF16)
    lo = (r - mid.astype(F32)).astype(BF16)
    return hi, mid, lo


GLA_SCAN = 256


def _gla_stages(q, k, la, v, gg, consts, st_ref, store):
    lt, km, vm, sm, gain = consts
    n = q.shape[0]
    c = GLA_CHUNK
    nc = n // c
    hc = GLA_HEADS * c

    parts = []
    for r in range(n // GLA_SCAN):
        hi, mid, lo = _split3(la[r * GLA_SCAN:(r + 1) * GLA_SCAN])
        parts.append(_dot(lt, hi) + _dot(lt, mid) + _dot(lt, lo))
    yield
    cum = jnp.concatenate(parts, axis=0)
    cum_last = jnp.broadcast_to(cum.reshape(nc, c, GLA_QK)[:, c - 1:c, :], (nc, c, GLA_QK)).reshape(n, GLA_QK)
    q_in = (q * (GLA_DK ** -0.5) * jnp.exp(cum)).astype(BF16)
    k_in = (k * jnp.exp(-cum)).astype(BF16)
    k_out = (k * jnp.exp(cum_last - cum)).astype(BF16)
    decay = jnp.exp(cum_last)
    vb = v.astype(BF16)
    causal = (lax.broadcasted_iota(jnp.int32, (c, hc), 1) % c
              <= lax.broadcasted_iota(jnp.int32, (c, hc), 0))
    yield
    intra, updates = [], []
    for i in range(nc):
        rows = slice(i * c, (i + 1) * c)
        k_bd =jnp.concatenate([k_in[rows]] * GLA_HEADS, axis=0) * km
        v_bd = jnp.concatenate([vb[rows]] * GLA_HEADS, axis=0) * vm
        ko_bd = jnp.concatenate([k_out[rows]] * GLA_HEADS, axis=0) * km
        v_rows = jnp.concatenate([vb[rows, h * GLA_DV:(h + 1) * GLA_DV] for h in range(GLA_HEADS)], axis=0)
        attn = jnp.where(causal, _dot_nt(q_in[rows], k_bd), 0.0).astype(BF16)
        intra.append(_dot(attn, v_bd))
        updates.append(_dot_tn(v_rows, ko_bd))
    yield
    st = st_ref[...]
    outs = []
    for i in range(nc):
        st_bd = jnp.concatenate([st.astype(BF16)] * GLA_HEADS, axis=0) * sm
        outs.append(intra[i] + _dot_nt(q_in[i * c:(i + 1) * c], st_bd))
        st = st * decay[i * c:i * c + 1] + updates[i]
    st_ref[...] = st
    yield
    o = jnp.concatenate(outs, axis=0)
    y = jnp.concatenate([_rms(o[:, h * GLA_DV:(h + 1) * GLA_DV]) for h in range(GLA_HEADS)], axis=-1)
    store((y * gain * (gg * _sigmoid(gg))).astype(BF16))


def _gla_constants():
    c = GLA_CHUNK
    r = jnp.arange(GLA_SCAN)
    lt = ((r[:, None] // c == r[None, :] // c) & (r[:, None] >= r[None, :])).astype(BF16)
    rh = jnp.arange(GLA_HEADS * c) // c
    km = (rh[:, None] == jnp.arange(GLA_QK)[None, :] // GLA_DK).astype(BF16)
    vm = (rh[:, None] == jnp.arange(GLA_V)[None, :] // GLA_DV).astype(BF16)
    sm = (jnp.arange(GLA_V)[:, None] // GLA_DV == jnp.arange(GLA_QK)[None, :] // GLA_DK).astype(BF16)
    return lt, km, vm, sm


FOX_ZERO_EXP = 106.0
FOX_GROUP = 16


def _fox_kernel(js_ref, q_ref, k_ref, v_ref, cum_ref, mask_ref, o_ref, va_ref, m_ref, acc_ref, *, tq, tk, nq):
    bh = pl.program_id(0) * FOX_HEADS + pl.program_id(1)
    win = tk + tq
    lane = lax.broadcasted_iota(jnp.int32, (v_ref.shape[1], FOX_HD), 1)
    va_ref[:, 0:FOX_HD] = v_ref[0]
    va_ref[:, FOX_HD:] = jnp.where(lane == 0, 1.0, 0.0).astype(BF16)

    def load_q(i):
        q0 = pl.multiple_of(i * tq, tq)
        q = q_ref[0, pl.ds(q0, tq), :]
        origin = cum_ref[0, 0, 0:1, pl.ds(q0, tq)][:, 0:1]
        return q0, q, origin

    def window_scores(i):
        q0, q, origin = load_q(i)
        k0 = pl.multiple_of(jnp.maximum(q0 - tk, 0), tk)
        s = _dot_nt(q, k_ref[0, pl.ds(k0, win), :]) + (origin - cum_ref[0, 0, 0:1, pl.ds(k0, win)])
        return s + mask_ref[jnp.minimum(i, 1)], k0

    def window_softmax(slot, s, k0):
        m = jnp.max(s, axis=-1, keepdims=True)
        m_ref[slot] = m
        acc_ref[slot] = _dot(jnp.exp2(s - m).astype(BF16), va_ref[pl.ds(k0, win), :])

    def earlier(slot, i):
        first = js_ref[bh * nq + i]
        last = i * (tq // tk) - 1

        @pl.when(first < last)
        def _():
            _, q, origin = load_q(i)

            def body(j, _):
                k0 = pl.multiple_of(j * tk, tk)
                s = _dot_nt(q, k_ref[0, pl.ds(k0, tk), :]) + (origin - cum_ref[0, 0, 0:1, pl.ds(k0, tk)])
                m = m_ref[slot]
                m_new = jnp.maximum(m, jnp.max(s, axis=-1, keepdims=True))
                p = jnp.exp2(s - m_new).astype(BF16)
                acc_ref[slot] = jnp.exp2(m - m_new) * acc_ref[slot] + _dot(p, va_ref[pl.ds(k0, tk), :])
                m_ref[slot] = m_new
                return 0

            lax.fori_loop(first, last, body, 0)

    def finish(slot, i):
        q0 = pl.multiple_of(i * tq, tq)
        acc = acc_ref[slot]
        o_ref[0, pl.ds(q0, tq), :] = (acc[:, :FOX_HD] / acc[:, FOX_HD:FOX_HD + 1]).astype(o_ref.dtype)

    def step(n, _):
        blocks = [n * FOX_GROUP + slot for slot in range(FOX_GROUP)]
        scores = [window_scores(i) for i in blocks]
        for slot, (s, k0) in enumerate(scores):
            window_softmax(slot, s, k0)
        for slot, i in enumerate(blocks):
            earlier(slot, i)
        for slot, i in enumerate(blocks):
            finish(slot, i)
        return 0

    lax.fori_loop(0, nq // FOX_GROUP, step, 0)


def _fox_first_block(cum, g_q, g_k, tq, tk):
    c = cum[:, :, 0, :] / LOG2E
    first = c[:, :, 0::tq]
    last = c[:, :, tk - 1::tk]
    nq, nk = first.shape[-1], last.shape[-1]
    bound = 2.0 * 1.02 * (FOX_HD ** 0.5) * jnp.max(jnp.abs(g_q)) * jnp.max(jnp.abs(g_k))
    needed = first[..., :, None] - last[..., None, :] + bound >= -FOX_ZERO_EXP
    js = jnp.min(jnp.where(needed, jnp.arange(nk, dtype=jnp.int32), nk), axis=-1)
    window_start = jnp.maximum(jnp.arange(nq, dtype=jnp.int32) * (tq // tk) - 1, 0)
    return jnp.minimum(js, window_start).reshape(-1).astype(jnp.int32)


def _fox(fq, fk, fv, cum, g_q, g_k, *, tq, tk):
    b, s, _ = fq.shape
    nq = s // tq
    assert nq % FOX_GROUP == 0 and tq % tk == 0 and s >= tq + tk
    js = _fox_first_block(cum, g_q, g_k, tq, tk)
    rel = jnp.arange(tk + tq)[None, :] - jnp.arange(tq)[:, None]
    mask = jnp.stack([jnp.where(rel <= 0, 0.0, -jnp.inf), jnp.where(rel <= tk, 0.0, -jnp.inf)]).astype(F32)
    blk = lambda bi, h, js_ref: (bi, 0, h)
    grid_spec = pltpu.PrefetchScalarGridSpec(
        num_scalar_prefetch=1,
        grid=(b, FOX_HEADS),
        in_specs=[
            pl.BlockSpec((1, s, FOX_HD), blk),
            pl.BlockSpec((1, s, FOX_HD), blk),
            pl.BlockSpec((1, s, FOX_HD), blk),
            pl.BlockSpec((1, 1, SUBLANES, s), lambda bi, h, js_ref: (bi, h, 0, 0)),
            pl.BlockSpec((2, tq, tk + tq), lambda bi, h, js_ref: (0, 0, 0)),
        ],
        out_specs=pl.BlockSpec((1, s, FOX_HD), blk),
        scratch_shapes=[
            pltpu.VMEM((s, 2 * FOX_HD), BF16),
            pltpu.VMEM((FOX_GROUP, tq, 1), F32),
            pltpu.VMEM((FOX_GROUP, tq, 2 * FOX_HD), F32),
        ],
    )
    return pl.pallas_call(
        functools.partial(_fox_kernel, tq=tq, tk=tk, nq=nq),
        grid_spec=grid_spec,
        out_shape=jax.ShapeDtypeStruct((b, s, FOX_W), BF16),
        compiler_params=_params(("arbitrary", "arbitrary")),
        name="fox",
    )(js, fq, fk, fv, cum, mask)


def _merge_kernel(x_ref, g_ref, wg_ref, bg_ref, yg_ref, yf_ref, mq_ref, mk_ref, mv_ref,
                  wb_ref, wo_ref, o_ref, *, tm, sub):
    d = x_ref.shape[-1]

    def sub_tile(r0, n):
        rows = slice(r0, r0 + n)
        x = x_ref[0, rows, :]
        h = (_rms(x) * g_ref[0]).astype(BF16)

        def gate_logits(i):
            return _dot(h, wg_ref[0, :, i * d:(i + 1) * d]) + bg_ref[0, :, i * d:(i + 1) * d]

        def head(hd):
            return slice(hd * MEM_HD, (hd + 1) * MEM_HD)

        yield
        scores = [_dot_nt(mq_ref[0, rows, head(hd)], mk_ref[0, 0, :, head(hd)])
                  for hd in range(MEM_HEADS)]
        g0, t0 = gate_logits(0), _dot(yg_ref[0, rows, :], wb_ref[0, 0])
        yield
        probs = []
        for s in scores:
            p = jnp.exp(s - jnp.max(s, axis=-1, keepdims=True))
            probs.append((p / jnp.sum(p, axis=-1, keepdims=True)).astype(BF16))
        g1, t1 = gate_logits(1), _dot(yf_ref[0, rows, :], wb_ref[0, 1])
        yield
        merged = _sigmoid(g0) * t0
        y_mem = jnp.concatenate([_dot(p, mv_ref[0, 0, :, head(hd)]) for hd, p in enumerate(probs)],
                                axis=-1).astype(BF16)
        g2 = gate_logits(2)
        yield
        merged = merged + _sigmoid(g1) * t1
        t2 = _dot(y_mem, wb_ref[0, 2])
        yield
        merged = merged + _sigmoid(g2) * t2
        o_ref[0, rows, :] = x + _dot(merged.astype(BF16), wo_ref[0])

    _run_staggered([sub_tile(r0, sub) for r0 in range(0, tm, sub)], lead=2)


def _merge(x, g_mix, w_gate, b_gate, y_gla, y_fox, mq, mk, mv, w_branch, w_out, *, layer, tm, sub):
    b, s, d = x.shape
    m = mk.shape[2]
    tok = lambda i, j: (i, j, 0)
    return pl.pallas_call(
        functools.partial(_merge_kernel, tm=tm, sub=sub),
        grid=(b, s // tm),
        in_specs=[
            pl.BlockSpec((1, tm, d), tok),
            _layer_spec(g_mix, layer),
            _layer_spec(w_gate, layer),
            _layer_spec(b_gate, layer),
            pl.BlockSpec((1, tm, GLA_V), tok),
            pl.BlockSpec((1, tm, FOX_W), tok),
            pl.BlockSpec((1, tm, MEM_W), tok),
            pl.BlockSpec((1, 1, m, MEM_W), lambda i, j: (layer, i, 0, 0)),
            pl.BlockSpec((1, 1, m, MEM_W), lambda i, j: (layer, i, 0, 0)),
            _layer_spec(w_branch, layer),
            _layer_spec(w_out, layer),
        ],
        out_specs=pl.BlockSpec((1, tm, d), tok),
        out_shape=jax.ShapeDtypeStruct((b, s, d), F32),
        compiler_params=_params(("arbitrary", "arbitrary")),
        name="merge",
    )(x, g_mix, w_gate, b_gate, y_gla, y_fox, mq, mk, mv, w_branch, w_out)


def _ffn_kernel(x_ref, g_ref, wg_ref, wu_ref, wd_ref, o_ref, *, tm, sub):
    def sub_tile(r0, n):
        rows = slice(r0, r0 + n)
        x = x_ref[0, rows, :]
        h = (_rms(x) * g_ref[0]).astype(BF16)
        yield
        gate = _dot(h, wg_ref[0])
        up = _dot(h, wu_ref[0])
        yield
        act = (gate * _sigmoid(gate) * up).astype(BF16)
        yield
        o_ref[0, rows, :] = x + _dot(act, wd_ref[0])

    _run_staggered([sub_tile(r0, sub) for r0 in range(0, tm, sub)], lead=2)


def _ffn(x, g_ffn, w_gate, w_up, w_down, *, layer, tm, sub):
    b, s, d = x.shape
    tok = lambda i, j: (i, j, 0)
    return pl.pallas_call(
        functools.partial(_ffn_kernel, tm=tm, sub=sub),
        grid=(b, s // tm),
        in_specs=[
            pl.BlockSpec((1, tm, d), tok),
            _layer_spec(g_ffn, layer),
            _layer_spec(w_gate, layer),
            _layer_spec(w_up, layer),
            _layer_spec(w_down, layer),
        ],
        out_specs=pl.BlockSpec((1, tm, d), tok),
        out_shape=jax.ShapeDtypeStruct((b, s, d), F32),
        compiler_params=_params(("arbitrary", "arbitrary")),
        name="ffn",
    )(x, g_ffn, w_gate, w_up, w_down)


def kernel(x, mem, g_mix, w_in, w_gla_a2, b_gla_a, g_gla_out, b_fox_f, g_fox_q, g_fox_k, g_mem, w_mem_kv, g_mem_q, g_mem_k, b_gate, w_branch, w_out, g_ffn, w_ffn_gate, w_ffn_up, w_ffn_down):
    depth, d = g_mix.shape
    o_ga = 2 * GLA_QK + 2 * GLA_V
    o_fox = o_ga + GLA_RANK
    o_ff = o_fox + 3 * FOX_W
    o_mq = o_ff + FOX_HEADS
    o_bg = o_mq + MEM_W

    mk, mv = _memkv(mem, _rows(g_mem), w_mem_kv.astype(BF16), _rows(g_mem_k))
    w_a, w_g = _wprep(w_in, (o_ga, o_fox, o_ff, o_mq, o_bg), rb=WPREP_ROWS)
    w2p = jnp.zeros((depth, SMALL_W, GLA_QK), F32).at[:, SMALL_GA:SMALL_GA + GLA_RANK].set(w_gla_a2).astype(BF16)
    b_small = jnp.zeros((depth, 1, SMALL_W), F32).at[:, 0, SMALL_FF:SMALL_FF + FOX_HEADS].set(b_fox_f)
    w_br, w_o = w_branch.astype(BF16), w_out.astype(BF16)
    w_fg, w_fu, w_fd = w_ffn_gate.astype(BF16), w_ffn_up.astype(BF16), w_ffn_down.astype(BF16)
    g_mix3, b_a3, g_go3, g_fq3, g_fk3, g_mq3, b_g3, g_ffn3 = map(
        _rows, (g_mix, b_gla_a, g_gla_out, g_fox_q, g_fox_k, g_mem_q, b_gate, g_ffn))

    for l in range(depth):
        y_gla, fq, fk, fv, mq, cum = _proj(x, g_mix3, w_a, w2p, b_a3, b_small, g_fq3, g_fk3, g_mq3, g_go3,
                                           layer=l, tm=TOKEN_TILE, sub=PROJ_SUB)
        y_fox = _fox(fq, fk, fv, cum, g_fox_q[l], g_fox_k[l], tq=FOX_TQ, tk=FOX_TK)
        x = _merge(x, g_mix3, w_g, b_g3, y_gla, y_fox, mq, mk, mv, w_br, w_o,
                   layer=l, tm=TOKEN_TILE, sub=MERGE_SUB)
        x = _ffn(x, g_ffn3, w_fg, w_fu, w_fd, layer=l, tm=TOKEN_TILE, sub=FFN_SUB)
    return x
```

```python
import functools

import jax
import jax.numpy as jnp
from jax import lax
from jax.experimental import pallas as pl
from jax.experimental.pallas import tpu as pltpu

EPS = 1e-6
GLA_HEADS = 4
GLA_DK = 64
GLA_DV = 128
GLA_RANK = 16
GLA_TAU = 16.0
GLA_CHUNK = 64
GLA_QK = GLA_HEADS * GLA_DK
GLA_V = GLA_HEADS * GLA_DV
FOX_HEADS = 4
FOX_HD = 128
FOX_W = FOX_HEADS * FOX_HD
MEM_HEADS = 4
MEM_HD = 128
MEM_W = MEM_HEADS * MEM_HD
N_BRANCH = 3
LOG2E = 1.4426950408889634

LANES = 128
SUBLANES = 8
VMEM_LIMIT_BYTES = 60000 * 1024

TOKEN_TILE = 1024
PROJ_SUB = 512
MERGE_SUB = 1024
FFN_SUB = 256
FOX_TQ = 256
FOX_TK = 256
WPREP_ROWS = 256
WTRANS_COLS = 256

SMALL_W = LANES
SMALL_FF = 0
SMALL_GA = SUBLANES

BF16 = jnp.bfloat16
F32 = jnp.float32


def _dot(a, b):
    return jnp.dot(a, b, preferred_element_type=F32)


def _dot_nt(a, b):
    return lax.dot_general(a, b, (((1,), (1,)), ((), ())), preferred_element_type=F32)


def _dot_tn(a, b):
    return lax.dot_general(a, b, (((0,), (0,)), ((), ())), preferred_element_type=F32)


def _rms(x):
    return x * lax.rsqrt(jnp.mean(x * x, axis=-1, keepdims=True) + EPS)


def _log_sigmoid(x):
    return jnp.minimum(x, 0.0) - jnp.log1p(jnp.exp(-jnp.abs(x)))


def _sigmoid(x):
    return 1.0 / (1.0 + jnp.exp(-x))


def _head_rms(x, gain, heads, width):
    outs = []
    for h in range(heads):
        outs.append(_rms(x[:, h * width:(h + 1) * width]) * gain)
    return jnp.concatenate(outs, axis=-1)


def _params(sem):
    return pltpu.CompilerParams(dimension_semantics=sem, vmem_limit_bytes=VMEM_LIMIT_BYTES)


def _layer_spec(arr, layer):
    zeros = (0,) * (arr.ndim - 1)
    return pl.BlockSpec((1,) + arr.shape[1:], lambda i, j: (layer,) + zeros, pipeline_mode=pl.Buffered(1))


def _run_staggered(tiles, lead):
    live = {k: t for k, t in enumerate(tiles)}
    step = 0
    while live:
        for k in sorted(live):
            if step >= k * lead and next(live[k], StopIteration) is StopIteration:
                del live[k]
        step += 1


def _rows(p):
    return p.reshape(p.shape[0], 1, p.shape[1])


def _memkv_kernel(mem_ref, g_ref, w_ref, gk_ref, k_ref, v_ref):
    h = (_rms(mem_ref[0]) * g_ref[0]).astype(BF16)
    kv = _dot(h, w_ref[0])
    k_ref[0, 0] = _head_rms(kv[:, :MEM_W], gk_ref[0], MEM_HEADS, MEM_HD).astype(BF16)
    v_ref[0, 0] = kv[:, MEM_W:].astype(BF16)


def _memkv(mem, g_mem, w_kv, g_k):
    b, m, d = mem.shape
    depth = w_kv.shape[0]
    out = jax.ShapeDtypeStruct((depth, b, m, MEM_W), BF16)
    return pl.pallas_call(
        _memkv_kernel,
        grid=(depth, b),
        in_specs=[
            pl.BlockSpec((1, m, d), lambda l, i: (i, 0, 0)),
            pl.BlockSpec((1, 1, d), lambda l, i: (l, 0, 0)),
            pl.BlockSpec((1, d, 2 * MEM_W), lambda l, i: (l, 0, 0)),
            pl.BlockSpec((1, 1, MEM_HD), lambda l, i: (l, 0, 0)),
        ],
        out_specs=[
            pl.BlockSpec((1, 1, m, MEM_W), lambda l, i: (l, i, 0, 0)),
            pl.BlockSpec((1, 1, m, MEM_W), lambda l, i: (l, i, 0, 0)),
        ],
        out_shape=[out, out],
        compiler_params=_params(("arbitrary", "arbitrary")),
        name="memkv",
    )(mem, g_mem, w_kv, g_k)


def _wprep_kernel(w_ref, wa_ref, wg_ref, *, offs):
    o_ga, o_fox, o_ff, o_mq, o_bg, width = offs
    n_main = o_ga
    w = w_ref[0]
    rows = w.shape[0]
    wa_ref[0, :, 0:n_main] = w[:, 0:n_main].astype(BF16)
    wa_ref[0, :, n_main:n_main + 3 * FOX_W] = w[:, o_fox:o_ff].astype(BF16)
    wa_ref[0, :, n_main + 3 * FOX_W:n_main + 3 * FOX_W + MEM_W] = w[:, o_mq:o_bg].astype(BF16)
    small = jnp.concatenate([
        w[:, o_ff:o_mq], jnp.zeros((rows, SMALL_GA - FOX_HEADS), w.dtype),
        w[:, o_ga:o_fox], jnp.zeros((rows, SMALL_W - SMALL_GA - GLA_RANK), w.dtype)], axis=1)
    wa_ref[0, :, n_main + 3 * FOX_W + MEM_W:] = small.astype(BF16)
    wg_ref[0] = w[:, o_bg:width].astype(BF16)


def _wtrans_kernel(w_ref, o_ref):
    for l in range(w_ref.shape[1]):
        o_ref[l] = w_ref[:, l, :].T.astype(o_ref.dtype)


def _wtrans(w_in, *, cb):
    depth, d, width = w_in.shape
    nb = pl.cdiv(width, cb)
    return pl.pallas_call(
        _wtrans_kernel,
        grid=(nb,),
        in_specs=[pl.BlockSpec((cb, depth, d), lambda j: (j, 0, 0))],
        out_specs=pl.BlockSpec((depth, d, cb), lambda j: (0, 0, j)),
        out_shape=jax.ShapeDtypeStruct((depth, d, nb * cb), BF16),
        compiler_params=_params(("arbitrary",)),
        name="wtrans",
    )(jnp.transpose(w_in, (2, 0, 1)))


def _wprep(w, offs, *, width, rb):
    depth, d, padded = w.shape
    o_ga, o_fox, o_ff, o_mq, o_bg = offs
    wa = o_ga + 3 * FOX_W + MEM_W + SMALL_W
    wg = width - o_bg
    return pl.pallas_call(
        functools.partial(_wprep_kernel, offs=offs + (width,)),
        grid=(depth, d // rb),
        in_specs=[pl.BlockSpec((1, rb, padded), lambda l, r: (l, r, 0))],
        out_specs=[pl.BlockSpec((1, rb, wa), lambda l, r: (l, r, 0)),
                   pl.BlockSpec((1, rb, wg), lambda l, r: (l, r, 0))],
        out_shape=[jax.ShapeDtypeStruct((depth, d, wa), BF16), jax.ShapeDtypeStruct((depth, d, wg), BF16)],
        compiler_params=_params(("arbitrary", "arbitrary")),
        name="wprep",
    )(w)


def _lane_cumsum(x):
    lane = lax.broadcasted_iota(jnp.int32, x.shape, 1)
    shift = 1
    while shift < LANES:
        x = x + jnp.where(lane >= shift, pltpu.roll(x, shift, 1), 0.0)
        shift *= 2
    return x


def _proj_kernel(x_ref, g_ref, w_ref, w2_ref, ba_ref, bs_ref, gfq_ref, gfk_ref, gmq_ref, ggo_ref,
                 lt_ref, km_ref, vm_ref, sm_ref,
                 yg_ref, fq_ref, fk_ref, fv_ref, mq_ref, cum_ref, carry_ref, st_ref, *, tm, sub):
    @pl.when(pl.program_id(1) == 0)
    def _():
        carry_ref[...] = jnp.zeros_like(carry_ref)
        st_ref[...] = jnp.zeros_like(st_ref)

    o_gla, o_fox, o_mq, o_small = 0, 2 * GLA_QK + 2 * GLA_V, 2 * GLA_QK + 2 * GLA_V + 3 * FOX_W, \
        2 * GLA_QK + 2 * GLA_V + 3 * FOX_W + MEM_W
    gla_consts = (lt_ref[...], km_ref[...], vm_ref[...], sm_ref[...], ggo_ref[0])

    def sub_tile(r0, n):
        rows = slice(r0, r0 + n)
        h = (_rms(x_ref[0, rows, :]) * g_ref[0]).astype(BF16)
        yield
        small = _dot(h, w_ref[0, :, o_small:o_small + SMALL_W])
        p_qk = _dot(h, w_ref[0, :, o_gla:o_gla + 2 * GLA_QK])
        yield
        a_logit = _dot(small.astype(BF16), w2_ref[0])
        p_vg = _dot(h, w_ref[0, :, o_gla + 2 * GLA_QK:o_fox])
        yield
        log_a = _log_sigmoid(a_logit + ba_ref[0]) / GLA_TAU

        def store_gla(y):
            yg_ref[0, rows, :] = y

        gla = _gla_stages(p_qk[:, :GLA_QK], p_qk[:, GLA_QK:], log_a, p_vg[:, :GLA_V], p_vg[:, GLA_V:],
                          gla_consts, st_ref, store_gla)
        p_fq = _dot(h, w_ref[0, :, o_fox:o_fox + FOX_W])
        next(gla)
        yield
        p_fk = _dot(h, w_ref[0, :, o_fox + FOX_W:o_fox + 2 * FOX_W])
        fq_ref[0, rows, :] = (_head_rms(p_fq, gfq_ref[0], FOX_HEADS, FOX_HD)
                              * (FOX_HD ** -0.5 * LOG2E)).astype(BF16)
        next(gla)
        yield
        p_fv = _dot(h, w_ref[0, :, o_fox + 2 * FOX_W:o_mq])
        fk_ref[0, rows, :] = _head_rms(p_fk, gfk_ref[0], FOX_HEADS, FOX_HD).astype(BF16)
        next(gla)
        yield
        p_mq = _dot(h, w_ref[0, :, o_mq:o_small])
        fv_ref[0, rows, :] = p_fv.astype(BF16)
        next(gla)
        yield
        next(gla, None)
        log_f = _log_sigmoid(small + bs_ref[0])
        log_ft = log_f.T[0:SUBLANES, :]
        carry = carry_ref[...]
        blocks = []
        for j in range(n // LANES):
            c = _lane_cumsum(log_ft[:, j * LANES:(j + 1) * LANES]) + carry
            blocks.append(c)
            carry = jnp.broadcast_to(c[:, LANES - 1:LANES], carry.shape)
        carry_ref[...] = carry
        cum = jnp.concatenate(blocks, axis=-1) * LOG2E
        for hd in range(FOX_HEADS):
            cum_ref[0, hd, :, rows] = jnp.broadcast_to(cum[hd:hd + 1, :], (SUBLANES, n))
        yield
        mq_ref[0, rows, :] = (_head_rms(p_mq, gmq_ref[0], MEM_HEADS, MEM_HD) * (MEM_HD ** -0.5)).astype(BF16)

    _run_staggered([sub_tile(r0, sub) for r0 in range(0, tm, sub)], lead=3)


def _proj(x, g_mix, w_a, w2p, b_a, b_small, g_fq, g_fk, g_mq, g_go, *, layer, tm, sub):
    b, s, d = x.shape
    assert sub % GLA_SCAN == 0
    tok = lambda i, j: (i, j, 0)
    stacked = (g_mix, w_a, w2p, b_a, b_small, g_fq, g_fk, g_mq, g_go)
    consts = _gla_constants()
    return pl.pallas_call(
        functools.partial(_proj_kernel, tm=tm, sub=sub),
        grid=(b, s // tm),
        in_specs=([pl.BlockSpec((1, tm, d), tok)] + [_layer_spec(p, layer) for p in stacked]
                  + [pl.BlockSpec(c.shape, lambda i, j: (0, 0), pipeline_mode=pl.Buffered(1)) for c in consts]),
        out_specs=[
            pl.BlockSpec((1, tm, GLA_V), tok),
            pl.BlockSpec((1, tm, FOX_W), tok),
            pl.BlockSpec((1, tm, FOX_W), tok),
            pl.BlockSpec((1, tm, FOX_W), tok),
            pl.BlockSpec((1, tm, MEM_W), tok),
            pl.BlockSpec((1, FOX_HEADS, SUBLANES, tm), lambda i, j: (i, 0, 0, j)),
        ],
        out_shape=[
            jax.ShapeDtypeStruct((b, s, GLA_V), BF16),
            jax.ShapeDtypeStruct((b, s, FOX_W), BF16),
            jax.ShapeDtypeStruct((b, s, FOX_W), BF16),
            jax.ShapeDtypeStruct((b, s, FOX_W), BF16),
            jax.ShapeDtypeStruct((b, s, MEM_W), BF16),
            jax.ShapeDtypeStruct((b, FOX_HEADS, SUBLANES, s), F32),
        ],
        scratch_shapes=[pltpu.VMEM((SUBLANES, LANES), F32), pltpu.VMEM((GLA_DV, GLA_QK), F32)],
        compiler_params=_params(("arbitrary", "arbitrary")),
        name="proj",
    )(x, *stacked, *consts)


def _split3(x):
    hi = x.astype(BF16)
    r = x - hi.astype(F32)
    mid = r.astype(BF16)
    lo = (r - mid.astype(F32)).astype(BF16)
    return hi, mid, lo


GLA_SCAN = 256


def _gla_stages(q, k, la, v, gg, consts, st_ref, store):
    lt, km, vm, sm, gain = consts
    n = q.shape[0]
    c = GLA_CHUNK
    nc = n // c
    hc = GLA_HEADS * c

    parts = []
    for r in range(n // GLA_SCAN):
        hi, mid, lo = _split3(la[r * GLA_SCAN:(r + 1) * GLA_SCAN])
        parts.append(_dot(lt, hi) + _dot(lt, mid) + _dot(lt, lo))
    yield
    cum = jnp.concatenate(parts, axis=0)
    cum_last = jnp.broadcast_to(cum.reshape(nc, c, GLA_QK)[:, c - 1:c, :], (nc, c, GLA_QK)).reshape(n, GLA_QK)
    q_in = (q * (GLA_DK ** -0.5) * jnp.exp(cum)).astype(BF16)
    k_in = (k * jnp.exp(-cum)).astype(BF16)
    k_out = (k * jnp.exp(cum_last - cum)).astype(BF16)
    decay = jnp.exp(cum_last)
    vb = v.astype(BF16)
    causal = (lax.broadcasted_iota(jnp.int32, (c, hc), 1) % c
              <= lax.broadcasted_iota(jnp.int32, (c, hc), 0))
    yield
    intra, updates = [], []
    for i in range(nc):
        rows = slice(i * c, (i + 1) * c)
        k_bd =jnp.concatenate([k_in[rows]] * GLA_HEADS, axis=0) * km
        v_bd = jnp.concatenate([vb[rows]] * GLA_HEADS, axis=0) * vm
        ko_bd = jnp.concatenate([k_out[rows]] * GLA_HEADS, axis=0) * km
        v_rows = jnp.concatenate([vb[rows, h * GLA_DV:(h + 1) * GLA_DV] for h in range(GLA_HEADS)], axis=0)
        attn = jnp.where(causal, _dot_nt(q_in[rows], k_bd), 0.0).astype(BF16)
        intra.append(_dot(attn, v_bd))
        updates.append(_dot_tn(v_rows, ko_bd))
    yield
    st = st_ref[...]
    outs = []
    for i in range(nc):
        st_bd = jnp.concatenate([st.astype(BF16)] * GLA_HEADS, axis=0) * sm
        outs.append(intra[i] + _dot_nt(q_in[i * c:(i + 1) * c], st_bd))
        st = st * decay[i * c:i * c + 1] + updates[i]
    st_ref[...] = st
    yield
    o = jnp.concatenate(outs, axis=0)
    y = jnp.concatenate([_rms(o[:, h * GLA_DV:(h + 1) * GLA_DV]) for h in range(GLA_HEADS)], axis=-1)
    store((y * gain * (gg * _sigmoid(gg))).astype(BF16))


def _gla_constants():
    c = GLA_CHUNK
    r = jnp.arange(GLA_SCAN)
    lt = ((r[:, None] // c == r[None, :] // c) & (r[:, None] >= r[None, :])).astype(BF16)
    rh = jnp.arange(GLA_HEADS * c) // c
    km = (rh[:, None] == jnp.arange(GLA_QK)[None, :] // GLA_DK).astype(BF16)
    vm = (rh[:, None] == jnp.arange(GLA_V)[None, :] // GLA_DV).astype(BF16)
    sm = (jnp.arange(GLA_V)[:, None] // GLA_DV == jnp.arange(GLA_QK)[None, :] // GLA_DK).astype(BF16)
    return lt, km, vm, sm


FOX_ZERO_EXP = 106.0
FOX_GROUP = 8


def _fox_kernel(js_ref, q_ref, k_ref, v_ref, cum_ref, mask_ref, o_ref, va_ref, m_ref, acc_ref, *, tq, tk, nq):
    bh = pl.program_id(0) * FOX_HEADS + pl.program_id(1)
    win = tk + tq
    lane = lax.broadcasted_iota(jnp.int32, (v_ref.shape[1], FOX_HD), 1)
    va_ref[:, 0:FOX_HD] = v_ref[0]
    va_ref[:, FOX_HD:] = jnp.where(lane == 0, 1.0, 0.0).astype(BF16)

    def load_q(i):
        q0 = pl.multiple_of(i * tq, tq)
        q = q_ref[0, pl.ds(q0, tq), :]
        origin = cum_ref[0, 0, 0:1, pl.ds(q0, tq)][:, 0:1]
        return q0, q, origin

    def window_scores(i):
        q0, q, origin = load_q(i)
        k0 = pl.multiple_of(jnp.maximum(q0 - tk, 0), tk)
        s = _dot_nt(q, k_ref[0, pl.ds(k0, win), :]) + (origin - cum_ref[0, 0, 0:1, pl.ds(k0, win)])
        return s + mask_ref[jnp.minimum(i, 1)], k0

    def window_softmax(slot, s, k0):
        m = jnp.max(s, axis=-1, keepdims=True)
        m_ref[slot] = m
        acc_ref[slot] = _dot(jnp.exp2(s - m).astype(BF16), va_ref[pl.ds(k0, win), :])

    def earlier(slot, i):
        first = js_ref[bh * nq + i]
        last = i * (tq // tk) - 1

        @pl.when(first < last)
        def _():
            _, q, origin = load_q(i)

            def body(j, _):
                k0 = pl.multiple_of(j * tk, tk)
                s = _dot_nt(q, k_ref[0, pl.ds(k0, tk), :]) + (origin - cum_ref[0, 0, 0:1, pl.ds(k0, tk)])
                m = m_ref[slot]
                m_new = jnp.maximum(m, jnp.max(s, axis=-1, keepdims=True))
                p = jnp.exp2(s - m_new).astype(BF16)
                acc_ref[slot] = jnp.exp2(m - m_new) * acc_ref[slot] + _dot(p, va_ref[pl.ds(k0, tk), :])
                m_ref[slot] = m_new
                return 0

            lax.fori_loop(first, last, body, 0)

    def finish(slot, i):
        q0 = pl.multiple_of(i * tq, tq)
        acc = acc_ref[slot]
        o_ref[0, pl.ds(q0, tq), :] = (acc[:, :FOX_HD] / acc[:, FOX_HD:FOX_HD + 1]).astype(o_ref.dtype)

    def step(n, _):
        blocks = [n * FOX_GROUP + slot for slot in range(FOX_GROUP)]
        scores = [window_scores(i) for i in blocks]
        for slot, (s, k0) in enumerate(scores):
            window_softmax(slot, s, k0)
        for slot, i in enumerate(blocks):
            earlier(slot, i)
        for slot, i in enumerate(blocks):
            finish(slot, i)
        return 0

    lax.fori_loop(0, nq // FOX_GROUP, step, 0)


def _fox_first_block(cum, g_q, g_k, tq, tk):
    c = cum[:, :, 0, :] / LOG2E
    first = c[:, :, 0::tq]
    last = c[:, :, tk - 1::tk]
    nq, nk = first.shape[-1], last.shape[-1]
    bound = 2.0 * 1.02 * (FOX_HD ** 0.5) * jnp.max(jnp.abs(g_q)) * jnp.max(jnp.abs(g_k))
    needed = first[..., :, None] - last[..., None, :] + bound >= -FOX_ZERO_EXP
    js = jnp.min(jnp.where(needed, jnp.arange(nk, dtype=jnp.int32), nk), axis=-1)
    window_start = jnp.maximum(jnp.arange(nq, dtype=jnp.int32) * (tq // tk) - 1, 0)
    return jnp.minimum(js, window_start).reshape(-1).astype(jnp.int32)


def _fox(fq, fk, fv, cum, g_q, g_k, *, tq, tk):
    b, s, _ = fq.shape
    nq = s // tq
    assert nq % FOX_GROUP == 0 and tq % tk == 0 and s >= tq + tk
    js = _fox_first_block(cum, g_q, g_k, tq, tk)
    rel = jnp.arange(tk + tq)[None, :] - jnp.arange(tq)[:, None]
    mask = jnp.stack([jnp.where(rel <= 0, 0.0, -jnp.inf), jnp.where(rel <= tk, 0.0, -jnp.inf)]).astype(F32)
    blk = lambda bi, h, js_ref: (bi, 0, h)
    grid_spec = pltpu.PrefetchScalarGridSpec(
        num_scalar_prefetch=1,
        grid=(b, FOX_HEADS),
        in_specs=[
            pl.BlockSpec((1, s, FOX_HD), blk),
            pl.BlockSpec((1, s, FOX_HD), blk),
            pl.BlockSpec((1, s, FOX_HD), blk),
            pl.BlockSpec((1, 1, SUBLANES, s), lambda bi, h, js_ref: (bi, h, 0, 0)),
            pl.BlockSpec((2, tq, tk + tq), lambda bi, h, js_ref: (0, 0, 0)),
        ],
        out_specs=pl.BlockSpec((1, s, FOX_HD), blk),
        scratch_shapes=[
            pltpu.VMEM((s, 2 * FOX_HD), BF16),
            pltpu.VMEM((FOX_GROUP, tq, 1), F32),
            pltpu.VMEM((FOX_GROUP, tq, 2 * FOX_HD), F32),
        ],
    )
    return pl.pallas_call(
        functools.partial(_fox_kernel, tq=tq, tk=tk, nq=nq),
        grid_spec=grid_spec,
        out_shape=jax.ShapeDtypeStruct((b, s, FOX_W), BF16),
        compiler_params=_params(("arbitrary", "arbitrary")),
        name="fox",
    )(js, fq, fk, fv, cum, mask)


def _merge_kernel(x_ref, g_ref, wg_ref, bg_ref, yg_ref, yf_ref, mq_ref, mk_ref, mv_ref,
                  wb_ref, wo_ref, o_ref, *, tm, sub):
    d = x_ref.shape[-1]

    def sub_tile(r0, n):
        rows = slice(r0, r0 + n)
        x = x_ref[0, rows, :]
        h = (_rms(x) * g_ref[0]).astype(BF16)

        def gate_logits(i):
            return _dot(h, wg_ref[0, :, i * d:(i + 1) * d]) + bg_ref[0, :, i * d:(i + 1) * d]

        def head(hd):
            return slice(hd * MEM_HD, (hd + 1) * MEM_HD)

        yield
        scores = [_dot_nt(mq_ref[0, rows, head(hd)], mk_ref[0, 0, :, head(hd)])
                  for hd in range(MEM_HEADS)]
        g0, t0 = gate_logits(0), _dot(yg_ref[0, rows, :], wb_ref[0, 0])
        yield
        probs = []
        for s in scores:
            p = jnp.exp(s - jnp.max(s, axis=-1, keepdims=True))
            probs.append((p / jnp.sum(p, axis=-1, keepdims=True)).astype(BF16))
        g1, t1 = gate_logits(1), _dot(yf_ref[0, rows, :], wb_ref[0, 1])
        yield
        merged = _sigmoid(g0) * t0
        y_mem = jnp.concatenate([_dot(p, mv_ref[0, 0, :, head(hd)]) for hd, p in enumerate(probs)],
                                axis=-1).astype(BF16)
        g2 = gate_logits(2)
        yield
        merged = merged + _sigmoid(g1) * t1
        t2 = _dot(y_mem, wb_ref[0, 2])
        yield
        merged = merged + _sigmoid(g2) * t2
        o_ref[0, rows, :] = x + _dot(merged.astype(BF16), wo_ref[0])

    _run_staggered([sub_tile(r0, sub) for r0 in range(0, tm, sub)], lead=2)


def _merge(x, g_mix, w_gate, b_gate, y_gla, y_fox, mq, mk, mv, w_branch, w_out, *, layer, tm, sub):
    b, s, d = x.shape
    m = mk.shape[2]
    tok = lambda i, j: (i, j, 0)
    return pl.pallas_call(
        functools.partial(_merge_kernel, tm=tm, sub=sub),
        grid=(b, s // tm),
        in_specs=[
            pl.BlockSpec((1, tm, d), tok),
            _layer_spec(g_mix, layer),
            _layer_spec(w_gate, layer),
            _layer_spec(b_gate, layer),
            pl.BlockSpec((1, tm, GLA_V), tok),
            pl.BlockSpec((1, tm, FOX_W), tok),
            pl.BlockSpec((1, tm, MEM_W), tok),
            pl.BlockSpec((1, 1, m, MEM_W), lambda i, j: (layer, i, 0, 0)),
            pl.BlockSpec((1, 1, m, MEM_W), lambda i, j: (layer, i, 0, 0)),
            _layer_spec(w_branch, layer),
            _layer_spec(w_out, layer),
        ],
        out_specs=pl.BlockSpec((1, tm, d), tok),
        out_shape=jax.ShapeDtypeStruct((b, s, d), F32),
        compiler_params=_params(("arbitrary", "arbitrary")),
        name="merge",
    )(x, g_mix, w_gate, b_gate, y_gla, y_fox, mq, mk, mv, w_branch, w_out)


def _ffn_kernel(x_ref, g_ref, wg_ref, wu_ref, wd_ref, o_ref, *, tm, sub):
    def sub_tile(r0, n):
        rows = slice(r0, r0 + n)
        x = x_ref[0, rows, :]
        h = (_rms(x) * g_ref[0]).astype(BF16)
        yield
        gate = _dot(h, wg_ref[0])
        up = _dot(h, wu_ref[0])
        yield
        act = (gate * _sigmoid(gate) * up).astype(BF16)
        yield
        o_ref[0, rows, :] = x + _dot(act, wd_ref[0])

    _run_staggered([sub_tile(r0, sub) for r0 in range(0, tm, sub)], lead=2)


def _ffn(x, g_ffn, w_gate, w_up, w_down, *, layer, tm, sub):
    b, s, d = x.shape
    tok = lambda i, j: (i, j, 0)
    return pl.pallas_call(
        functools.partial(_ffn_kernel, tm=tm, sub=sub),
        grid=(b, s // tm),
        in_specs=[
            pl.BlockSpec((1, tm, d), tok),
            _layer_spec(g_ffn, layer),
            _layer_spec(w_gate, layer),
            _layer_spec(w_up, layer),
            _layer_spec(w_down, layer),
        ],
        out_specs=pl.BlockSpec((1, tm, d), tok),
        out_shape=jax.ShapeDtypeStruct((b, s, d), F32),
        compiler_params=_params(("arbitrary", "arbitrary")),
        name="ffn",
    )(x, g_ffn, w_gate, w_up, w_down)


def kernel(x, mem, g_mix, w_in, w_gla_a2, b_gla_a, g_gla_out, b_fox_f, g_fox_q, g_fox_k, g_mem, w_mem_kv, g_mem_q, g_mem_k, b_gate, w_branch, w_out, g_ffn, w_ffn_gate, w_ffn_up, w_ffn_down):
    depth, d = g_mix.shape
    o_ga = 2 * GLA_QK + 2 * GLA_V
    o_fox = o_ga + GLA_RANK
    o_ff = o_fox + 3 * FOX_W
    o_mq = o_ff + FOX_HEADS
    o_bg = o_mq + MEM_W

    mk, mv = _memkv(mem, _rows(g_mem), w_mem_kv.astype(BF16), _rows(g_mem_k))
    w_a, w_g = _wprep(_wtrans(w_in, cb=WTRANS_COLS), (o_ga, o_fox, o_ff, o_mq, o_bg),
                      width=w_in.shape[-1], rb=WPREP_ROWS)
    w2p = jnp.zeros((depth, SMALL_W, GLA_QK), F32).at[:, SMALL_GA:SMALL_GA + GLA_RANK].set(w_gla_a2).astype(BF16)
    b_small = jnp.zeros((depth, 1, SMALL_W), F32).at[:, 0, SMALL_FF:SMALL_FF + FOX_HEADS].set(b_fox_f)
    w_br, w_o = w_branch.astype(BF16), w_out.astype(BF16)
    w_fg, w_fu, w_fd = w_ffn_gate.astype(BF16), w_ffn_up.astype(BF16), w_ffn_down.astype(BF16)
    g_mix3, b_a3, g_go3, g_fq3, g_fk3, g_mq3, b_g3, g_ffn3 = map(
        _rows, (g_mix, b_gla_a, g_gla_out, g_fox_q, g_fox_k, g_mem_q, b_gate, g_ffn))

    for l in range(depth):
        y_gla, fq, fk, fv, mq, cum = _proj(x, g_mix3, w_a, w2p, b_a3, b_small, g_fq3, g_fk3, g_mq3, g_go3,
                                           layer=l, tm=TOKEN_TILE, sub=PROJ_SUB)
        y_fox = _fox(fq, fk, fv, cum, g_fox_q[l], g_fox_k[l], tq=FOX_TQ, tk=FOX_TK)
        x = _merge(x, g_mix3, w_g, b_g3, y_gla, y_fox, mq, mk, mv, w_br, w_o,
                   layer=l, tm=TOKEN_TILE, sub=MERGE_SUB)
        x = _ffn(x, g_ffn3, w_fg, w_fu, w_fd, layer=l, tm=TOKEN_TILE, sub=FFN_SUB)
    return x
```

```python
import functools

import jax
import jax.numpy as jnp
from jax import lax
from jax.experimental import pallas as pl
from jax.experimental.pallas import tpu as pltpu

EPS = 1e-6
GLA_HEADS = 4
GLA_DK = 64
GLA_DV = 128
GLA_RANK = 16
GLA_TAU = 16.0
GLA_CHUNK = 64
GLA_QK = GLA_HEADS * GLA_DK
GLA_V = GLA_HEADS * GLA_DV
FOX_HEADS = 4
FOX_HD = 128
FOX_W = FOX_HEADS * FOX_HD
MEM_HEADS = 4
MEM_HD = 128
MEM_W = MEM_HEADS * MEM_HD
N_BRANCH = 3
LOG2E = 1.4426950408889634

LANES = 128
SUBLANES = 8
VMEM_LIMIT_BYTES = 60000 * 1024

TOKEN_TILE = 1024
PROJ_SUB = 512
MERGE_SUB = 1024
FFN_SUB = 256
FOX_TQ = 256
FOX_TK = 256
WPREP_ROWS = 256
WTRANS_COLS = 256

SMALL_W = LANES
SMALL_FF = 0
SMALL_GA = SUBLANES

BF16 = jnp.bfloat16
F32 = jnp.float32


def _dot(a, b):
    return jnp.dot(a, b, preferred_element_type=F32)


def _dot_nt(a, b):
    return lax.dot_general(a, b, (((1,), (1,)), ((), ())), preferred_element_type=F32)


def _dot_tn(a, b):
    return lax.dot_general(a, b, (((0,), (0,)), ((), ())), preferred_element_type=F32)


def _rms(x):
    return x * lax.rsqrt(jnp.mean(x * x, axis=-1, keepdims=True) + EPS)


def _log_sigmoid(x):
    return jnp.minimum(x, 0.0) - jnp.log1p(jnp.exp(-jnp.abs(x)))


def _sigmoid(x):
    return 1.0 / (1.0 + jnp.exp(-x))


def _head_rms(x, gain, heads, width):
    outs = []
    for h in range(heads):
        outs.append(_rms(x[:, h * width:(h + 1) * width]) * gain)
    return jnp.concatenate(outs, axis=-1)


def _params(sem):
    return pltpu.CompilerParams(dimension_semantics=sem, vmem_limit_bytes=VMEM_LIMIT_BYTES)


def _layer_spec(arr, layer):
    zeros = (0,) * (arr.ndim - 1)
    return pl.BlockSpec((1,) + arr.shape[1:], lambda i, j: (layer,) + zeros, pipeline_mode=pl.Buffered(1))


def _run_staggered(tiles, lead):
    live = {k: t for k, t in enumerate(tiles)}
    step = 0
    while live:
        for k in sorted(live):
            if step >= k * lead and next(live[k], StopIteration) is StopIteration:
                del live[k]
        step += 1


def _rows(p):
    return p.reshape(p.shape[0], 1, p.shape[1])


def _memkv_kernel(mem_ref, g_ref, w_ref, gk_ref, k_ref, v_ref):
    h = (_rms(mem_ref[...]) * g_ref[0]).astype(BF16)
    kv = _dot(h, w_ref[0].astype(BF16))
    k_ref[0] = _head_rms(kv[:, :MEM_W], gk_ref[0], MEM_HEADS, MEM_HD).astype(BF16)
    v_ref[0] = kv[:, MEM_W:].astype(BF16)


def _memkv(mem, g_mem, w_kv, g_k):
    b, m, d = mem.shape
    depth = w_kv.shape[0]
    out = jax.ShapeDtypeStruct((depth, b * m, MEM_W), BF16)
    layer = lambda l: (l, 0, 0)
    mk, mv = pl.pallas_call(
        _memkv_kernel,
        grid=(depth,),
        in_specs=[
            pl.BlockSpec((b * m, d), lambda l: (0, 0)),
            pl.BlockSpec((1, 1, d), layer),
            pl.BlockSpec((1, d, 2 * MEM_W), layer),
            pl.BlockSpec((1, 1, MEM_HD), layer),
        ],
        out_specs=[pl.BlockSpec((1, b * m, MEM_W), layer), pl.BlockSpec((1, b * m, MEM_W), layer)],
        out_shape=[out, out],
        compiler_params=_params(("arbitrary",)),
        name="memkv",
    )(mem.reshape(b * m, d), g_mem, w_kv, g_k)
    return mk.reshape(depth, b, m, MEM_W), mv.reshape(depth, b, m, MEM_W)


def _wprep_kernel(w_ref, wa_ref, wg_ref, *, offs):
    o_ga, o_fox, o_ff, o_mq, o_bg, width = offs
    n_main = o_ga
    w = w_ref[0]
    rows = w.shape[0]
    wa_ref[0, :, 0:n_main] = w[:, 0:n_main].astype(BF16)
    wa_ref[0, :, n_main:n_main + 3 * FOX_W] = w[:, o_fox:o_ff].astype(BF16)
    wa_ref[0, :, n_main + 3 * FOX_W:n_main + 3 * FOX_W + MEM_W] = w[:, o_mq:o_bg].astype(BF16)
    small = jnp.concatenate([
        w[:, o_ff:o_mq], jnp.zeros((rows, SMALL_GA - FOX_HEADS), w.dtype),
        w[:, o_ga:o_fox], jnp.zeros((rows, SMALL_W - SMALL_GA - GLA_RANK), w.dtype)], axis=1)
    wa_ref[0, :, n_main + 3 * FOX_W + MEM_W:] = small.astype(BF16)
    wg_ref[0] = w[:, o_bg:width].astype(BF16)


def _wtrans_kernel(w_ref, o_ref):
    for l in range(w_ref.shape[1]):
        o_ref[l] = w_ref[:, l, :].T.astype(o_ref.dtype)


def _wtrans(w_in, *, cb):
    depth, d, width = w_in.shape
    nb = pl.cdiv(width, cb)
    return pl.pallas_call(
        _wtrans_kernel,
        grid=(nb,),
        in_specs=[pl.BlockSpec((cb, depth, d), lambda j: (j, 0, 0))],
        out_specs=pl.BlockSpec((depth, d, cb), lambda j: (0, 0, j)),
        out_shape=jax.ShapeDtypeStruct((depth, d, nb * cb), BF16),
        compiler_params=_params(("arbitrary",)),
        name="wtrans",
    )(jnp.transpose(w_in, (2, 0, 1)))


def _wprep(w, offs, *, width, rb):
    depth, d, padded = w.shape
    o_ga, o_fox, o_ff, o_mq, o_bg = offs
    wa = o_ga + 3 * FOX_W + MEM_W + SMALL_W
    wg = width - o_bg
    return pl.pallas_call(
        functools.partial(_wprep_kernel, offs=offs + (width,)),
        grid=(depth, d // rb),
        in_specs=[pl.BlockSpec((1, rb, padded), lambda l, r: (l, r, 0))],
        out_specs=[pl.BlockSpec((1, rb, wa), lambda l, r: (l, r, 0)),
                   pl.BlockSpec((1, rb, wg), lambda l, r: (l, r, 0))],
        out_shape=[jax.ShapeDtypeStruct((depth, d, wa), BF16), jax.ShapeDtypeStruct((depth, d, wg), BF16)],
        compiler_params=_params(("arbitrary", "arbitrary")),
        name="wprep",
    )(w)


def _lane_cumsum(x):
    lane = lax.broadcasted_iota(jnp.int32, x.shape, 1)
    shift = 1
    while shift < LANES:
        x = x + jnp.where(lane >= shift, pltpu.roll(x, shift, 1), 0.0)
        shift *= 2
    return x


def _proj_kernel(x_ref, g_ref, w_ref, w2_ref, ba_ref, bs_ref, gfq_ref, gfk_ref, gmq_ref, ggo_ref,
                 lt_ref, km_ref, vm_ref, sm_ref,
                 yg_ref, fq_ref, fk_ref, fv_ref, mq_ref, cum_ref, carry_ref, st_ref, *, tm, sub):
    @pl.when(pl.program_id(1) == 0)
    def _():
        carry_ref[...] = jnp.zeros_like(carry_ref)
        st_ref[...] = jnp.zeros_like(st_ref)

    o_gla, o_fox, o_mq, o_small = 0, 2 * GLA_QK + 2 * GLA_V, 2 * GLA_QK + 2 * GLA_V + 3 * FOX_W, \
        2 * GLA_QK + 2 * GLA_V + 3 * FOX_W + MEM_W
    gla_consts = (lt_ref[...], km_ref[...], vm_ref[...], sm_ref[...], ggo_ref[0])

    def sub_tile(r0, n):
        rows = slice(r0, r0 + n)
        h = (_rms(x_ref[0, rows, :]) * g_ref[0]).astype(BF16)
        yield
        small = _dot(h, w_ref[0, :, o_small:o_small + SMALL_W])
        p_qk = _dot(h, w_ref[0, :, o_gla:o_gla + 2 * GLA_QK])
        yield
        a_logit = _dot(small.astype(BF16), w2_ref[0])
        p_vg = _dot(h, w_ref[0, :, o_gla + 2 * GLA_QK:o_fox])
        yield
        log_a = _log_sigmoid(a_logit + ba_ref[0]) / GLA_TAU

        def store_gla(y):
            yg_ref[0, rows, :] = y

        gla = _gla_stages(p_qk[:, :GLA_QK], p_qk[:, GLA_QK:], log_a, p_vg[:, :GLA_V], p_vg[:, GLA_V:],
                          gla_consts, st_ref, store_gla)
        p_fq = _dot(h, w_ref[0, :, o_fox:o_fox + FOX_W])
        next(gla)
        yield
        p_fk = _dot(h, w_ref[0, :, o_fox + FOX_W:o_fox + 2 * FOX_W])
        fq_ref[0, rows, :] = (_head_rms(p_fq, gfq_ref[0], FOX_HEADS, FOX_HD)
                              * (FOX_HD ** -0.5 * LOG2E)).astype(BF16)
        next(gla)
        yield
        p_fv = _dot(h, w_ref[0, :, o_fox + 2 * FOX_W:o_mq])
        fk_ref[0, rows, :] = _head_rms(p_fk, gfk_ref[0], FOX_HEADS, FOX_HD).astype(BF16)
        next(gla)
        yield
        p_mq = _dot(h, w_ref[0, :, o_mq:o_small])
        fv_ref[0, rows, :] = p_fv.astype(BF16)
        next(gla)
        yield
        next(gla, None)
        log_f = _log_sigmoid(small + bs_ref[0])
        log_ft = log_f.T[0:SUBLANES, :]
        carry = carry_ref[...]
        blocks = []
        for j in range(n // LANES):
            c = _lane_cumsum(log_ft[:, j * LANES:(j + 1) * LANES]) + carry
            blocks.append(c)
            carry = jnp.broadcast_to(c[:, LANES - 1:LANES], carry.shape)
        carry_ref[...] = carry
        cum = jnp.concatenate(blocks, axis=-1) * LOG2E
        for hd in range(FOX_HEADS):
            cum_ref[0, hd, :, rows] = jnp.broadcast_to(cum[hd:hd + 1, :], (SUBLANES, n))
        yield
        mq_ref[0, rows, :] = (_head_rms(p_mq, gmq_ref[0], MEM_HEADS, MEM_HD) * (MEM_HD ** -0.5)).astype(BF16)

    _run_staggered([sub_tile(r0, sub) for r0 in range(0, tm, sub)], lead=3)


def _proj(x, g_mix, w_a, w2p, b_a, b_small, g_fq, g_fk, g_mq, g_go, *, layer, tm, sub):
    b, s, d = x.shape
    assert sub % GLA_SCAN == 0
    tok = lambda i, j: (i, j, 0)
    stacked = (g_mix, w_a, w2p, b_a, b_small, g_fq, g_fk, g_mq, g_go)
    consts = _gla_constants()
    return pl.pallas_call(
        functools.partial(_proj_kernel, tm=tm, sub=sub),
        grid=(b, s // tm),
        in_specs=([pl.BlockSpec((1, tm, d), tok)] + [_layer_spec(p, layer) for p in stacked]
                  + [pl.BlockSpec(c.shape, lambda i, j: (0, 0), pipeline_mode=pl.Buffered(1)) for c in consts]),
        out_specs=[
            pl.BlockSpec((1, tm, GLA_V), tok),
            pl.BlockSpec((1, tm, FOX_W), tok),
            pl.BlockSpec((1, tm, FOX_W), tok),
            pl.BlockSpec((1, tm, FOX_W), tok),
            pl.BlockSpec((1, tm, MEM_W), tok),
            pl.BlockSpec((1, FOX_HEADS, SUBLANES, tm), lambda i, j: (i, 0, 0, j)),
        ],
        out_shape=[
            jax.ShapeDtypeStruct((b, s, GLA_V), BF16),
            jax.ShapeDtypeStruct((b, s, FOX_W), BF16),
            jax.ShapeDtypeStruct((b, s, FOX_W), BF16),
            jax.ShapeDtypeStruct((b, s, FOX_W), BF16),
            jax.ShapeDtypeStruct((b, s, MEM_W), BF16),
            jax.ShapeDtypeStruct((b, FOX_HEADS, SUBLANES, s), F32),
        ],
        scratch_shapes=[pltpu.VMEM((SUBLANES, LANES), F32), pltpu.VMEM((GLA_DV, GLA_QK), F32)],
        compiler_params=_params(("arbitrary", "arbitrary")),
        name="proj",
    )(x, *stacked, *consts)


def _split3(x):
    hi = x.astype(BF16)
    r = x - hi.astype(F32)
    mid = r.astype(BF16)
    lo = (r - mid.astype(F32)).astype(BF16)
    return hi, mid, lo


GLA_SCAN = 256


def _gla_stages(q, k, la, v, gg, consts, st_ref, store):
    lt, km, vm, sm, gain = consts
    n = q.shape[0]
    c = GLA_CHUNK
    nc = n // c
    hc = GLA_HEADS * c

    parts = []
    for r in range(n // GLA_SCAN):
        hi, mid, lo = _split3(la[r * GLA_SCAN:(r + 1) * GLA_SCAN])
        parts.append(_dot(lt, hi) + _dot(lt, mid) + _dot(lt, lo))
    yield
    cum = jnp.concatenate(parts, axis=0)
    cum_last = jnp.broadcast_to(cum.reshape(nc, c, GLA_QK)[:, c - 1:c, :], (nc, c, GLA_QK)).reshape(n, GLA_QK)
    q_in = (q * (GLA_DK ** -0.5) * jnp.exp(cum)).astype(BF16)
    k_in = (k * jnp.exp(-cum)).astype(BF16)
    k_out = (k * jnp.exp(cum_last - cum)).astype(BF16)
    decay = jnp.exp(cum_last)
    vb = v.astype(BF16)
    causal = (lax.broadcasted_iota(jnp.int32, (c, hc), 1) % c
              <= lax.broadcasted_iota(jnp.int32, (c, hc), 0))
    yield
    def chunk_matmuls(chunks):
        intra, updates = [], []
        for i in chunks:
            rows = slice(i * c, (i + 1) * c)
            k_bd = jnp.concatenate([k_in[rows]] * GLA_HEADS, axis=0) * km
            v_bd = jnp.concatenate([vb[rows]] * GLA_HEADS, axis=0) * vm
            ko_bd = jnp.concatenate([k_out[rows]] * GLA_HEADS, axis=0) * km
            v_rows = jnp.concatenate([vb[rows, h * GLA_DV:(h + 1) * GLA_DV] for h in range(GLA_HEADS)], axis=0)
            attn = jnp.where(causal, _dot_nt(q_in[rows], k_bd), 0.0).astype(BF16)
            intra.append(_dot(attn, v_bd))
            updates.append(_dot_tn(v_rows, ko_bd))
        return intra, updates

    def state_chain(chunks, intra, updates):
        st = st_ref[...]
        outs = []
        for i, o_intra, upd in zip(chunks, intra, updates):
            st_bd = jnp.concatenate([st.astype(BF16)] * GLA_HEADS, axis=0) * sm
            outs.append(o_intra + _dot_nt(q_in[i * c:(i + 1) * c], st_bd))
            st = st * decay[i * c:i * c + 1] + upd
        st_ref[...] = st
        return outs

    first, second = range(0, nc // 2), range(nc // 2, nc)
    ready = chunk_matmuls(first)
    yield
    outs = state_chain(first, *ready)
    ready = chunk_matmuls(second)
    yield
    outs += state_chain(second, *ready)
    o = jnp.concatenate(outs, axis=0)
    y = jnp.concatenate([_rms(o[:, h * GLA_DV:(h + 1) * GLA_DV]) for h in range(GLA_HEADS)], axis=-1)
    store((y * gain * (gg * _sigmoid(gg))).astype(BF16))


def _gla_constants():
    c = GLA_CHUNK
    r = jnp.arange(GLA_SCAN)
    lt = ((r[:, None] // c == r[None, :] // c) & (r[:, None] >= r[None, :])).astype(BF16)
    rh = jnp.arange(GLA_HEADS * c) // c
    km = (rh[:, None] == jnp.arange(GLA_QK)[None, :] // GLA_DK).astype(BF16)
    vm = (rh[:, None] == jnp.arange(GLA_V)[None, :] // GLA_DV).astype(BF16)
    sm = (jnp.arange(GLA_V)[:, None] // GLA_DV == jnp.arange(GLA_QK)[None, :] // GLA_DK).astype(BF16)
    return lt, km, vm, sm


FOX_ZERO_EXP = 106.0
FOX_GROUP = 8


def _fox_kernel(js_ref, q_ref, k_ref, v_ref, cum_ref, mask_ref, o_ref, va_ref, m_ref, acc_ref, *, tq, tk, nq):
    bh = pl.program_id(0) * FOX_HEADS + pl.program_id(1)
    win = tk + tq
    lane = lax.broadcasted_iota(jnp.int32, (v_ref.shape[1], FOX_HD), 1)
    va_ref[:, 0:FOX_HD] = v_ref[0]
    va_ref[:, FOX_HD:] = jnp.where(lane == 0, 1.0, 0.0).astype(BF16)

    def load_q(i):
        q0 = pl.multiple_of(i * tq, tq)
        q = q_ref[0, pl.ds(q0, tq), :]
        origin = cum_ref[0, 0, 0:1, pl.ds(q0, tq)][:, 0:1]
        return q0, q, origin

    def window_scores(i):
        q0, q, origin = load_q(i)
        k0 = pl.multiple_of(jnp.maximum(q0 - tk, 0), tk)
        s = _dot_nt(q, k_ref[0, pl.ds(k0, win), :]) + (origin - cum_ref[0, 0, 0:1, pl.ds(k0, win)])
        return s + mask_ref[jnp.minimum(i, 1)], k0

    def window_softmax(slot, s, k0):
        m = jnp.max(s, axis=-1, keepdims=True)
        m_ref[slot] = m
        acc_ref[slot] = _dot(jnp.exp2(s - m).astype(BF16), va_ref[pl.ds(k0, win), :])

    def earlier(slot, i):
        first = js_ref[bh * nq + i]
        last = i * (tq // tk) - 1

        @pl.when(first < last)
        def _():
            _, q, origin = load_q(i)

            def body(j, _):
                k0 = pl.multiple_of(j * tk, tk)
                s = _dot_nt(q, k_ref[0, pl.ds(k0, tk), :]) + (origin - cum_ref[0, 0, 0:1, pl.ds(k0, tk)])
                m = m_ref[slot]
                m_new = jnp.maximum(m, jnp.max(s, axis=-1, keepdims=True))
                p = jnp.exp2(s - m_new).astype(BF16)
                acc_ref[slot] = jnp.exp2(m - m_new) * acc_ref[slot] + _dot(p, va_ref[pl.ds(k0, tk), :])
                m_ref[slot] = m_new
                return 0

            lax.fori_loop(first, last, body, 0)

    def finish(slot, i):
        q0 = pl.multiple_of(i * tq, tq)
        acc = acc_ref[slot]
        o_ref[0, pl.ds(q0, tq), :] = (acc[:, :FOX_HD] / acc[:, FOX_HD:FOX_HD + 1]).astype(o_ref.dtype)

    def step(n, _):
        blocks = [n * FOX_GROUP + slot for slot in range(FOX_GROUP)]
        scores = [window_scores(i) for i in blocks]
        for slot, (s, k0) in enumerate(scores):
            window_softmax(slot, s, k0)
        for slot, i in enumerate(blocks):
            earlier(slot, i)
        for slot, i in enumerate(blocks):
            finish(slot, i)
        return 0

    lax.fori_loop(0, nq // FOX_GROUP, step, 0)


def _fox_first_block(cum, g_q, g_k, tq, tk):
    c = cum[:, :, 0, :] / LOG2E
    first = c[:, :, 0::tq]
    last = c[:, :, tk - 1::tk]
    nq, nk = first.shape[-1], last.shape[-1]
    bound = 2.0 * 1.02 * (FOX_HD ** 0.5) * jnp.max(jnp.abs(g_q)) * jnp.max(jnp.abs(g_k))
    needed = first[..., :, None] - last[..., None, :] + bound >= -FOX_ZERO_EXP
    js = jnp.min(jnp.where(needed, jnp.arange(nk, dtype=jnp.int32), nk), axis=-1)
    window_start = jnp.maximum(jnp.arange(nq, dtype=jnp.int32) * (tq // tk) - 1, 0)
    return jnp.minimum(js, window_start).reshape(-1).astype(jnp.int32)


def _fox(fq, fk, fv, cum, g_q, g_k, *, tq, tk):
    b, s, _ = fq.shape
    nq = s // tq
    assert nq % FOX_GROUP == 0 and tq % tk == 0 and s >= tq + tk
    js = _fox_first_block(cum, g_q, g_k, tq, tk)
    rel = jnp.arange(tk + tq)[None, :] - jnp.arange(tq)[:, None]
    mask = jnp.stack([jnp.where(rel <= 0, 0.0, -jnp.inf), jnp.where(rel <= tk, 0.0, -jnp.inf)]).astype(F32)
    blk = lambda bi, h, js_ref: (bi, 0, h)
    grid_spec = pltpu.PrefetchScalarGridSpec(
        num_scalar_prefetch=1,
        grid=(b, FOX_HEADS),
        in_specs=[
            pl.BlockSpec((1, s, FOX_HD), blk),
            pl.BlockSpec((1, s, FOX_HD), blk),
            pl.BlockSpec((1, s, FOX_HD), blk),
            pl.BlockSpec((1, 1, SUBLANES, s), lambda bi, h, js_ref: (bi, h, 0, 0)),
            pl.BlockSpec((2, tq, tk + tq), lambda bi, h, js_ref: (0, 0, 0)),
        ],
        out_specs=pl.BlockSpec((1, s, FOX_HD), blk),
        scratch_shapes=[
            pltpu.VMEM((s, 2 * FOX_HD), BF16),
            pltpu.VMEM((FOX_GROUP, tq, 1), F32),
            pltpu.VMEM((FOX_GROUP, tq, 2 * FOX_HD), F32),
        ],
    )
    return pl.pallas_call(
        functools.partial(_fox_kernel, tq=tq, tk=tk, nq=nq),
        grid_spec=grid_spec,
        out_shape=jax.ShapeDtypeStruct((b, s, FOX_W), BF16),
        compiler_params=_params(("arbitrary", "arbitrary")),
        name="fox",
    )(js, fq, fk, fv, cum, mask)


def _merge_kernel(x_ref, g_ref, wg_ref, bg_ref, yg_ref, yf_ref, mq_ref, mk_ref, mv_ref,
                  wb_ref, wo_ref, o_ref, *, tm, sub):
    d = x_ref.shape[-1]

    def sub_tile(r0, n):
        rows = slice(r0, r0 + n)
        x = x_ref[0, rows, :]
        h = (_rms(x) * g_ref[0]).astype(BF16)

        def gate_logits(i):
            return _dot(h, wg_ref[0, :, i * d:(i + 1) * d]) + bg_ref[0, :, i * d:(i + 1) * d]

        def head(hd):
            return slice(hd * MEM_HD, (hd + 1) * MEM_HD)

        yield
        scores = [_dot_nt(mq_ref[0, rows, head(hd)], mk_ref[0, 0, :, head(hd)])
                  for hd in range(MEM_HEADS)]
        g0, t0 = gate_logits(0), _dot(yg_ref[0, rows, :], wb_ref[0, 0])
        yield
        probs = []
        for s in scores:
            p = jnp.exp(s - jnp.max(s, axis=-1, keepdims=True))
            probs.append((p / jnp.sum(p, axis=-1, keepdims=True)).astype(BF16))
        g1, t1 = gate_logits(1), _dot(yf_ref[0, rows, :], wb_ref[0, 1])
        yield
        merged = _sigmoid(g0) * t0
        y_mem = jnp.concatenate([_dot(p, mv_ref[0, 0, :, head(hd)]) for hd, p in enumerate(probs)],
                                axis=-1).astype(BF16)
        g2 = gate_logits(2)
        yield
        merged = merged + _sigmoid(g1) * t1
        t2 = _dot(y_mem, wb_ref[0, 2])
        yield
        merged = merged + _sigmoid(g2) * t2
        o_ref[0, rows, :] = x + _dot(merged.astype(BF16), wo_ref[0])

    _run_staggered([sub_tile(r0, sub) for r0 in range(0, tm, sub)], lead=2)


def _merge(x, g_mix, w_gate, b_gate, y_gla, y_fox, mq, mk, mv, w_branch, w_out, *, layer, tm, sub):
    b, s, d = x.shape
    m = mk.shape[2]
    tok = lambda i, j: (i, j, 0)
    return pl.pallas_call(
        functools.partial(_merge_kernel, tm=tm, sub=sub),
        grid=(b, s // tm),
        in_specs=[
            pl.BlockSpec((1, tm, d), tok),
            _layer_spec(g_mix, layer),
            _layer_spec(w_gate, layer),
            _layer_spec(b_gate, layer),
            pl.BlockSpec((1, tm, GLA_V), tok),
            pl.BlockSpec((1, tm, FOX_W), tok),
            pl.BlockSpec((1, tm, MEM_W), tok),
            pl.BlockSpec((1, 1, m, MEM_W), lambda i, j: (layer, i, 0, 0)),
            pl.BlockSpec((1, 1, m, MEM_W), lambda i, j: (layer, i, 0, 0)),
            _layer_spec(w_branch, layer),
            _layer_spec(w_out, layer),
        ],
        out_specs=pl.BlockSpec((1, tm, d), tok),
        out_shape=jax.ShapeDtypeStruct((b, s, d), F32),
        compiler_params=_params(("arbitrary", "arbitrary")),
        name="merge",
    )(x, g_mix, w_gate, b_gate, y_gla, y_fox, mq, mk, mv, w_branch, w_out)


def _ffn_kernel(x_ref, g_ref, wg_ref, wu_ref, wd_ref, o_ref, *, tm, sub):
    def sub_tile(r0, n):
        rows = slice(r0, r0 + n)
        x = x_ref[0, rows, :]
        h = (_rms(x) * g_ref[0]).astype(BF16)
        yield
        gate = _dot(h, wg_ref[0])
        up = _dot(h, wu_ref[0])
        yield
        act = (gate * _sigmoid(gate) * up).astype(BF16)
        yield
        o_ref[0, rows, :] = x + _dot(act, wd_ref[0])

    _run_staggered([sub_tile(r0, sub) for r0 in range(0, tm, sub)], lead=2)


def _ffn(x, g_ffn, w_gate, w_up, w_down, *, layer, tm, sub):
    b, s, d = x.shape
    tok = lambda i, j: (i, j, 0)
    return pl.pallas_call(
        functools.partial(_ffn_kernel, tm=tm, sub=sub),
        grid=(b, s // tm),
        in_specs=[
            pl.BlockSpec((1, tm, d), tok),
            _layer_spec(g_ffn, layer),
            _layer_spec(w_gate, layer),
            _layer_spec(w_up, layer),
            _layer_spec(w_down, layer),
        ],
        out_specs=pl.BlockSpec((1, tm, d), tok),
        out_shape=jax.ShapeDtypeStruct((b, s, d), F32),
        compiler_params=_params(("arbitrary", "arbitrary")),
        name="ffn",
    )(x, g_ffn, w_gate, w_up, w_down)


def kernel(x, mem, g_mix, w_in, w_gla_a2, b_gla_a, g_gla_out, b_fox_f, g_fox_q, g_fox_k, g_mem, w_mem_kv, g_mem_q, g_mem_k, b_gate, w_branch, w_out, g_ffn, w_ffn_gate, w_ffn_up, w_ffn_down):
    depth, d = g_mix.shape
    o_ga = 2 * GLA_QK + 2 * GLA_V
    o_fox = o_ga + GLA_RANK
    o_ff = o_fox + 3 * FOX_W
    o_mq = o_ff + FOX_HEADS
    o_bg = o_mq + MEM_W

    mk, mv = _memkv(mem, _rows(g_mem), w_mem_kv, _rows(g_mem_k))
    w_a, w_g = _wprep(_wtrans(w_in, cb=WTRANS_COLS), (o_ga, o_fox, o_ff, o_mq, o_bg),
                      width=w_in.shape[-1], rb=WPREP_ROWS)
    w2p = jnp.zeros((depth, SMALL_W, GLA_QK), F32).at[:, SMALL_GA:SMALL_GA + GLA_RANK].set(w_gla_a2).astype(BF16)
    b_small = jnp.zeros((depth, 1, SMALL_W), F32).at[:, 0, SMALL_FF:SMALL_FF + FOX_HEADS].set(b_fox_f)
    w_br, w_o = w_branch.astype(BF16), w_out.astype(BF16)
    w_fg, w_fu, w_fd = w_ffn_gate.astype(BF16), w_ffn_up.astype(BF16), w_ffn_down.astype(BF16)
    g_mix3, b_a3, g_go3, g_fq3, g_fk3, g_mq3, b_g3, g_ffn3 = map(
        _rows, (g_mix, b_gla_a, g_gla_out, g_fox_q, g_fox_k, g_mem_q, b_gate, g_ffn))

    for l in range(depth):
        y_gla, fq, fk, fv, mq, cum = _proj(x, g_mix3, w_a, w2p, b_a3, b_small, g_fq3, g_fk3, g_mq3, g_go3,
                                           layer=l, tm=TOKEN_TILE, sub=PROJ_SUB)
        y_fox = _fox(fq, fk, fv, cum, g_fox_q[l], g_fox_k[l], tq=FOX_TQ, tk=FOX_TK)
        x = _merge(x, g_mix3, w_g, b_g3, y_gla, y_fox, mq, mk, mv, w_br, w_o,
                   layer=l, tm=TOKEN_TILE, sub=MERGE_SUB)
        x = _ffn(x, g_ffn3, w_fg, w_fu, w_fd, layer=l, tm=TOKEN_TILE, sub=FFN_SUB)
    return x
```

```python
import functools

import jax
import jax.numpy as jnp
from jax import lax
from jax.experimental import pallas as pl
from jax.experimental.pallas import tpu as pltpu

EPS = 1e-6
GLA_HEADS = 4
GLA_DK = 64
GLA_DV = 128
GLA_RANK = 16
GLA_TAU = 16.0
GLA_CHUNK = 64
GLA_QK = GLA_HEADS * GLA_DK
GLA_V = GLA_HEADS * GLA_DV
FOX_HEADS = 4
FOX_HD = 128
FOX_W = FOX_HEADS * FOX_HD
MEM_HEADS = 4
MEM_HD = 128
MEM_W = MEM_HEADS * MEM_HD
N_BRANCH = 3
LOG2E = 1.4426950408889634

LANES = 128
SUBLANES = 8
VMEM_LIMIT_BYTES = 60000 * 1024

TOKEN_TILE = 1024
PROJ_SUB = 512
MERGE_SUB = 1024
FFN_SUB = 256
FOX_TQ = 256
FOX_TK = 256
WPREP_ROWS = 256
WTRANS_COLS = 256

SMALL_W = LANES
SMALL_FF = 0
SMALL_GA = SUBLANES

BF16 = jnp.bfloat16
F32 = jnp.float32


def _dot(a, b):
    return jnp.dot(a, b, preferred_element_type=F32)


def _dot_nt(a, b):
    return lax.dot_general(a, b, (((1,), (1,)), ((), ())), preferred_element_type=F32)


def _dot_tn(a, b):
    return lax.dot_general(a, b, (((0,), (0,)), ((), ())), preferred_element_type=F32)


def _rms(x):
    return x * lax.rsqrt(jnp.mean(x * x, axis=-1, keepdims=True) + EPS)


def _log_sigmoid(x):
    return jnp.minimum(x, 0.0) - jnp.log1p(jnp.exp(-jnp.abs(x)))


def _sigmoid(x):
    return 1.0 / (1.0 + jnp.exp(-x))


def _head_rms(x, gain, heads, width):
    outs = []
    for h in range(heads):
        outs.append(_rms(x[:, h * width:(h + 1) * width]) * gain)
    return jnp.concatenate(outs, axis=-1)


def _params(sem):
    return pltpu.CompilerParams(dimension_semantics=sem, vmem_limit_bytes=VMEM_LIMIT_BYTES)


def _layer_spec(arr, layer):
    zeros = (0,) * (arr.ndim - 1)
    return pl.BlockSpec((1,) + arr.shape[1:], lambda i, j: (layer,) + zeros, pipeline_mode=pl.Buffered(1))


def _run_staggered(tiles, lead):
    live = {k: t for k, t in enumerate(tiles)}
    step = 0
    while live:
        for k in sorted(live):
            if step >= k * lead and next(live[k], StopIteration) is StopIteration:
                del live[k]
        step += 1


def _rows(p):
    return p.reshape(p.shape[0], 1, p.shape[1])


def _memkv_kernel(mem_ref, g_ref, w_ref, gk_ref, k_ref, v_ref):
    h = (_rms(mem_ref[...]) * g_ref[0]).astype(BF16)
    kv = _dot(h, w_ref[0].astype(BF16))
    k_ref[0] = _head_rms(kv[:, :MEM_W], gk_ref[0], MEM_HEADS, MEM_HD).astype(BF16)
    v_ref[0] = kv[:, MEM_W:].astype(BF16)


def _memkv(mem, g_mem, w_kv, g_k):
    b, m, d = mem.shape
    depth = w_kv.shape[0]
    out = jax.ShapeDtypeStruct((depth, b * m, MEM_W), BF16)
    layer = lambda l: (l, 0, 0)
    mk, mv = pl.pallas_call(
        _memkv_kernel,
        grid=(depth,),
        in_specs=[
            pl.BlockSpec((b * m, d), lambda l: (0, 0)),
            pl.BlockSpec((1, 1, d), layer),
            pl.BlockSpec((1, d, 2 * MEM_W), layer),
            pl.BlockSpec((1, 1, MEM_HD), layer),
        ],
        out_specs=[pl.BlockSpec((1, b * m, MEM_W), layer), pl.BlockSpec((1, b * m, MEM_W), layer)],
        out_shape=[out, out],
        compiler_params=_params(("arbitrary",)),
        name="memkv",
    )(mem.reshape(b * m, d), g_mem, w_kv, g_k)
    return mk.reshape(depth, b, m, MEM_W), mv.reshape(depth, b, m, MEM_W)


def _wprep_kernel(w_ref, t_ref, wa_ref, wg_ref, *, offs):
    o_ga, o_fox, o_ff, o_mq, o_bg, width = offs
    n_main = o_ga
    w = w_ref[0]
    rows, head = w.shape
    tail = width - head
    aligned = (width - o_bg) // LANES * LANES - LANES
    wa_ref[0, :, 0:n_main] = w[:, 0:n_main].astype(BF16)
    wa_ref[0, :, n_main:n_main + 3 * FOX_W] = w[:, o_fox:o_ff].astype(BF16)
    wa_ref[0, :, n_main + 3 * FOX_W:n_main + 3 * FOX_W + MEM_W] = w[:, o_mq:o_bg].astype(BF16)
    small = jnp.concatenate([
        w[:, o_ff:o_mq], jnp.zeros((rows, SMALL_GA - FOX_HEADS), w.dtype),
        w[:, o_ga:o_fox], jnp.zeros((rows, SMALL_W - SMALL_GA - GLA_RANK), w.dtype)], axis=1)
    wa_ref[0, :, n_main + 3 * FOX_W + MEM_W:] = small.astype(BF16)
    wg_ref[0, :, 0:aligned] = w[:, o_bg:o_bg + aligned]
    wg_ref[0, :, aligned:] = jnp.concatenate([w[:, o_bg + aligned:head], t_ref[0, :, 0:tail]], axis=1)


def _wtrans_kernel(w_ref, o_ref):
    for l in range(w_ref.shape[1]):
        o_ref[l] = w_ref[:, l, :].T.astype(o_ref.dtype)


def _wtrans(w_in, *, cb):
    depth, d, width = w_in.shape
    nb = width // cb
    return pl.pallas_call(
        _wtrans_kernel,
        grid=(nb,),
        in_specs=[pl.BlockSpec((cb, depth, d), lambda j: (j, 0, 0))],
        out_specs=pl.BlockSpec((depth, d, cb), lambda j: (0, 0, j)),
        out_shape=jax.ShapeDtypeStruct((depth, d, nb * cb), BF16),
        compiler_params=_params(("arbitrary",)),
        name="wtrans",
    )(jnp.transpose(w_in, (2, 0, 1)))


def _wprep(w_in, offs, *, cb, rb):
    depth, d, width = w_in.shape
    head = width // cb * cb
    w_head = _wtrans(w_in, cb=cb)
    w_tail = jnp.pad(w_in[:, :, head:].astype(BF16), ((0, 0), (0, 0), (0, LANES - (width - head))))
    o_ga, o_fox, o_ff, o_mq, o_bg = offs
    assert o_bg <= head and 0 < width - head <= LANES
    wa = o_ga + 3 * FOX_W + MEM_W + SMALL_W
    wg = width - o_bg
    return pl.pallas_call(
        functools.partial(_wprep_kernel, offs=offs + (width,)),
        grid=(depth, d // rb),
        in_specs=[pl.BlockSpec((1, rb, head), lambda l, r: (l, r, 0)),
                  pl.BlockSpec((1, rb, LANES), lambda l, r: (l, r, 0))],
        out_specs=[pl.BlockSpec((1, rb, wa), lambda l, r: (l, r, 0)),
                   pl.BlockSpec((1, rb, wg), lambda l, r: (l, r, 0))],
        out_shape=[jax.ShapeDtypeStruct((depth, d, wa), BF16), jax.ShapeDtypeStruct((depth, d, wg), BF16)],
        compiler_params=_params(("arbitrary", "arbitrary")),
        name="wprep",
    )(w_head, w_tail)


def _lane_cumsum(x):
    lane = lax.broadcasted_iota(jnp.int32, x.shape, 1)
    shift = 1
    while shift < LANES:
        x = x + jnp.where(lane >= shift, pltpu.roll(x, shift, 1), 0.0)
        shift *= 2
    return x


def _proj_kernel(x_ref, g_ref, w_ref, w2_ref, ba_ref, bs_ref, gfq_ref, gfk_ref, gmq_ref, ggo_ref,
                 lt_ref, km_ref, vm_ref, sm_ref,
                 yg_ref, fq_ref, fk_ref, fv_ref, mq_ref, cum_ref, carry_ref, st_ref, *, tm, sub):
    @pl.when(pl.program_id(1) == 0)
    def _():
        carry_ref[...] = jnp.zeros_like(carry_ref)
        st_ref[...] = jnp.zeros_like(st_ref)

    o_gla, o_fox, o_mq, o_small = 0, 2 * GLA_QK + 2 * GLA_V, 2 * GLA_QK + 2 * GLA_V + 3 * FOX_W, \
        2 * GLA_QK + 2 * GLA_V + 3 * FOX_W + MEM_W
    gla_consts = (lt_ref[...], km_ref[...], vm_ref[...], sm_ref[...], ggo_ref[0])

    def sub_tile(r0, n):
        rows = slice(r0, r0 + n)
        h = (_rms(x_ref[0, rows, :]) * g_ref[0]).astype(BF16)
        yield
        small = _dot(h, w_ref[0, :, o_small:o_small + SMALL_W])
        p_qk = _dot(h, w_ref[0, :, o_gla:o_gla + 2 * GLA_QK])
        yield
        a_logit = _dot(small.astype(BF16), w2_ref[0])
        p_vg = _dot(h, w_ref[0, :, o_gla + 2 * GLA_QK:o_fox])
        yield
        log_a = _log_sigmoid(a_logit + ba_ref[0]) / GLA_TAU

        def store_gla(y):
            yg_ref[0, rows, :] = y

        gla = _gla_stages(p_qk[:, :GLA_QK], p_qk[:, GLA_QK:], log_a, p_vg[:, :GLA_V], p_vg[:, GLA_V:],
                          gla_consts, st_ref, store_gla)
        p_fq = _dot(h, w_ref[0, :, o_fox:o_fox + FOX_W])
        next(gla)
        yield
        p_fk = _dot(h, w_ref[0, :, o_fox + FOX_W:o_fox + 2 * FOX_W])
        fq_ref[0, rows, :] = (_head_rms(p_fq, gfq_ref[0], FOX_HEADS, FOX_HD)
                              * (FOX_HD ** -0.5 * LOG2E)).astype(BF16)
        next(gla)
        yield
        p_fv = _dot(h, w_ref[0, :, o_fox + 2 * FOX_W:o_mq])
        fk_ref[0, rows, :] = _head_rms(p_fk, gfk_ref[0], FOX_HEADS, FOX_HD).astype(BF16)
        next(gla)
        yield
        p_mq = _dot(h, w_ref[0, :, o_mq:o_small])
        fv_ref[0, rows, :] = p_fv.astype(BF16)
        next(gla)
        yield
        next(gla, None)
        log_f = _log_sigmoid(small + bs_ref[0])
        log_ft = log_f.T[0:SUBLANES, :]
        carry = carry_ref[...]
        blocks = []
        for j in range(n // LANES):
            c = _lane_cumsum(log_ft[:, j * LANES:(j + 1) * LANES]) + carry
            blocks.append(c)
            carry = jnp.broadcast_to(c[:, LANES - 1:LANES], carry.shape)
        carry_ref[...] = carry
        cum = jnp.concatenate(blocks, axis=-1) * LOG2E
        for hd in range(FOX_HEADS):
            cum_ref[0, hd, :, rows] = jnp.broadcast_to(cum[hd:hd + 1, :], (SUBLANES, n))
        yield
        mq_ref[0, rows, :] = (_head_rms(p_mq, gmq_ref[0], MEM_HEADS, MEM_HD) * (MEM_HD ** -0.5)).astype(BF16)

    _run_staggered([sub_tile(r0, sub) for r0 in range(0, tm, sub)], lead=3)


def _proj(x, g_mix, w_a, w2p, b_a, b_small, g_fq, g_fk, g_mq, g_go, *, layer, tm, sub):
    b, s, d = x.shape
    assert sub % GLA_SCAN == 0
    tok = lambda i, j: (i, j, 0)
    stacked = (g_mix, w_a, w2p, b_a, b_small, g_fq, g_fk, g_mq, g_go)
    consts = _gla_constants()
    return pl.pallas_call(
        functools.partial(_proj_kernel, tm=tm, sub=sub),
        grid=(b, s // tm),
        in_specs=([pl.BlockSpec((1, tm, d), tok)] + [_layer_spec(p, layer) for p in stacked]
                  + [pl.BlockSpec(c.shape, lambda i, j: (0, 0), pipeline_mode=pl.Buffered(1)) for c in consts]),
        out_specs=[
            pl.BlockSpec((1, tm, GLA_V), tok),
            pl.BlockSpec((1, tm, FOX_W), tok),
            pl.BlockSpec((1, tm, FOX_W), tok),
            pl.BlockSpec((1, tm, FOX_W), tok),
            pl.BlockSpec((1, tm, MEM_W), tok),
            pl.BlockSpec((1, FOX_HEADS, SUBLANES, tm), lambda i, j: (i, 0, 0, j)),
        ],
        out_shape=[
            jax.ShapeDtypeStruct((b, s, GLA_V), BF16),
            jax.ShapeDtypeStruct((b, s, FOX_W), BF16),
            jax.ShapeDtypeStruct((b, s, FOX_W), BF16),
            jax.ShapeDtypeStruct((b, s, FOX_W), BF16),
            jax.ShapeDtypeStruct((b, s, MEM_W), BF16),
            jax.ShapeDtypeStruct((b, FOX_HEADS, SUBLANES, s), F32),
        ],
        scratch_shapes=[pltpu.VMEM((SUBLANES, LANES), F32), pltpu.VMEM((GLA_DV, GLA_QK), F32)],
        compiler_params=_params(("arbitrary", "arbitrary")),
        name="proj",
    )(x, *stacked, *consts)


def _split3(x):
    hi = x.astype(BF16)
    r = x - hi.astype(F32)
    mid = r.astype(BF16)
    lo = (r - mid.astype(F32)).astype(BF16)
    return hi, mid, lo


GLA_SCAN = 256


def _gla_stages(q, k, la, v, gg, consts, st_ref, store):
    lt, km, vm, sm, gain = consts
    n = q.shape[0]
    c = GLA_CHUNK
    nc = n // c
    hc = GLA_HEADS * c

    parts = []
    for r in range(n // GLA_SCAN):
        hi, mid, lo = _split3(la[r * GLA_SCAN:(r + 1) * GLA_SCAN])
        parts.append(_dot(lt, hi) + _dot(lt, mid) + _dot(lt, lo))
    yield
    cum = jnp.concatenate(parts, axis=0)
    cum_last = jnp.broadcast_to(cum.reshape(nc, c, GLA_QK)[:, c - 1:c, :], (nc, c, GLA_QK)).reshape(n, GLA_QK)
    q_in = (q * (GLA_DK ** -0.5) * jnp.exp(cum)).astype(BF16)
    k_in = (k * jnp.exp(-cum)).astype(BF16)
    k_out = (k * jnp.exp(cum_last - cum)).astype(BF16)
    decay = jnp.exp(cum_last)
    vb = v.astype(BF16)
    causal = (lax.broadcasted_iota(jnp.int32, (c, hc), 1) % c
              <= lax.broadcasted_iota(jnp.int32, (c, hc), 0))
    yield
    def chunk_matmuls(chunks):
        intra, updates = [], []
        for i in chunks:
            rows = slice(i * c, (i + 1) * c)
            k_bd = jnp.concatenate([k_in[rows]] * GLA_HEADS, axis=0) * km
            v_bd = jnp.concatenate([vb[rows]] * GLA_HEADS, axis=0) * vm
            ko_bd = jnp.concatenate([k_out[rows]] * GLA_HEADS, axis=0) * km
            v_rows = jnp.concatenate([vb[rows, h * GLA_DV:(h + 1) * GLA_DV] for h in range(GLA_HEADS)], axis=0)
            attn = jnp.where(causal, _dot_nt(q_in[rows], k_bd), 0.0).astype(BF16)
            intra.append(_dot(attn, v_bd))
            updates.append(_dot_tn(v_rows, ko_bd))
        return intra, updates

    def state_chain(chunks, intra, updates):
        st = st_ref[...]
        outs = []
        for i, o_intra, upd in zip(chunks, intra, updates):
            st_bd = jnp.concatenate([st.astype(BF16)] * GLA_HEADS, axis=0) * sm
            outs.append(o_intra + _dot_nt(q_in[i * c:(i + 1) * c], st_bd))
            st = st * decay[i * c:i * c + 1] + upd
        st_ref[...] = st
        return outs

    first, second = range(0, nc // 2), range(nc // 2, nc)
    ready = chunk_matmuls(first)
    yield
    outs = state_chain(first, *ready)
    ready = chunk_matmuls(second)
    yield
    outs += state_chain(second, *ready)
    o = jnp.concatenate(outs, axis=0)
    y = jnp.concatenate([_rms(o[:, h * GLA_DV:(h + 1) * GLA_DV]) for h in range(GLA_HEADS)], axis=-1)
    store((y * gain * (gg * _sigmoid(gg))).astype(BF16))


def _gla_constants():
    c = GLA_CHUNK
    r = jnp.arange(GLA_SCAN)
    lt = ((r[:, None] // c == r[None, :] // c) & (r[:, None] >= r[None, :])).astype(BF16)
    rh = jnp.arange(GLA_HEADS * c) // c
    km = (rh[:, None] == jnp.arange(GLA_QK)[None, :] // GLA_DK).astype(BF16)
    vm = (rh[:, None] == jnp.arange(GLA_V)[None, :] // GLA_DV).astype(BF16)
    sm = (jnp.arange(GLA_V)[:, None] // GLA_DV == jnp.arange(GLA_QK)[None, :] // GLA_DK).astype(BF16)
    return lt, km, vm, sm


FOX_ZERO_EXP = 106.0
FOX_GROUP = 8


def _fox_kernel(js_ref, q_ref, k_ref, v_ref, cum_ref, mask_ref, o_ref, va_ref, m_ref, acc_ref, *, tq, tk, nq):
    bh = pl.program_id(0) * FOX_HEADS + pl.program_id(1)
    win = tk + tq
    lane = lax.broadcasted_iota(jnp.int32, (v_ref.shape[1], FOX_HD), 1)
    va_ref[:, 0:FOX_HD] = v_ref[0]
    va_ref[:, FOX_HD:] = jnp.where(lane == 0, 1.0, 0.0).astype(BF16)

    def load_q(i):
        q0 = pl.multiple_of(i * tq, tq)
        q = q_ref[0, pl.ds(q0, tq), :]
        origin = cum_ref[0, 0, 0:1, pl.ds(q0, tq)][:, 0:1]
        return q0, q, origin

    def window_scores(i):
        q0, q, origin = load_q(i)
        k0 = pl.multiple_of(jnp.maximum(q0 - tk, 0), tk)
        s = _dot_nt(q, k_ref[0, pl.ds(k0, win), :]) + (origin - cum_ref[0, 0, 0:1, pl.ds(k0, win)])
        return s + mask_ref[jnp.minimum(i, 1)], k0

    def window_softmax(slot, s, k0):
        m = jnp.max(s, axis=-1, keepdims=True)
        m_ref[slot] = m
        acc_ref[slot] = _dot(jnp.exp2(s - m).astype(BF16), va_ref[pl.ds(k0, win), :])

    def earlier(slot, i):
        first = js_ref[bh * nq + i]
        last = i * (tq // tk) - 1

        @pl.when(first < last)
        def _():
            _, q, origin = load_q(i)

            def body(j, _):
                k0 = pl.multiple_of(j * tk, tk)
                s = _dot_nt(q, k_ref[0, pl.ds(k0, tk), :]) + (origin - cum_ref[0, 0, 0:1, pl.ds(k0, tk)])
                m = m_ref[slot]
                m_new = jnp.maximum(m, jnp.max(s, axis=-1, keepdims=True))
                p = jnp.exp2(s - m_new).astype(BF16)
                acc_ref[slot] = jnp.exp2(m - m_new) * acc_ref[slot] + _dot(p, va_ref[pl.ds(k0, tk), :])
                m_ref[slot] = m_new
                return 0

            lax.fori_loop(first, last, body, 0)

    def finish(slot, i):
        q0 = pl.multiple_of(i * tq, tq)
        acc = acc_ref[slot]
        o_ref[0, pl.ds(q0, tq), :] = (acc[:, :FOX_HD] / acc[:, FOX_HD:FOX_HD + 1]).astype(o_ref.dtype)

    def step(n, _):
        blocks = [n * FOX_GROUP + slot for slot in range(FOX_GROUP)]
        scores = [window_scores(i) for i in blocks]
        for slot, (s, k0) in enumerate(scores):
            window_softmax(slot, s, k0)
        for slot, i in enumerate(blocks):
            earlier(slot, i)
        for slot, i in enumerate(blocks):
            finish(slot, i)
        return 0

    lax.fori_loop(0, nq // FOX_GROUP, step, 0)


def _fox_first_block(cum, g_q, g_k, tq, tk):
    c = cum[:, :, 0, :] / LOG2E
    first = c[:, :, 0::tq]
    last = c[:, :, tk - 1::tk]
    nq, nk = first.shape[-1], last.shape[-1]
    bound = 2.0 * 1.02 * (FOX_HD ** 0.5) * jnp.max(jnp.abs(g_q)) * jnp.max(jnp.abs(g_k))
    needed = first[..., :, None] - last[..., None, :] + bound >= -FOX_ZERO_EXP
    js = jnp.min(jnp.where(needed, jnp.arange(nk, dtype=jnp.int32), nk), axis=-1)
    window_start = jnp.maximum(jnp.arange(nq, dtype=jnp.int32) * (tq // tk) - 1, 0)
    return jnp.minimum(js, window_start).reshape(-1).astype(jnp.int32)


def _fox(fq, fk, fv, cum, g_q, g_k, *, tq, tk):
    b, s, _ = fq.shape
    nq = s // tq
    assert nq % FOX_GROUP == 0 and tq % tk == 0 and s >= tq + tk
    js = _fox_first_block(cum, g_q, g_k, tq, tk)
    rel = jnp.arange(tk + tq)[None, :] - jnp.arange(tq)[:, None]
    mask = jnp.stack([jnp.where(rel <= 0, 0.0, -jnp.inf), jnp.where(rel <= tk, 0.0, -jnp.inf)]).astype(F32)
    blk = lambda bi, h, js_ref: (bi, 0, h)
    grid_spec = pltpu.PrefetchScalarGridSpec(
        num_scalar_prefetch=1,
        grid=(b, FOX_HEADS),
        in_specs=[
            pl.BlockSpec((1, s, FOX_HD), blk),
            pl.BlockSpec((1, s, FOX_HD), blk),
            pl.BlockSpec((1, s, FOX_HD), blk),
            pl.BlockSpec((1, 1, SUBLANES, s), lambda bi, h, js_ref: (bi, h, 0, 0)),
            pl.BlockSpec((2, tq, tk + tq), lambda bi, h, js_ref: (0, 0, 0)),
        ],
        out_specs=pl.BlockSpec((1, s, FOX_HD), blk),
        scratch_shapes=[
            pltpu.VMEM((s, 2 * FOX_HD), BF16),
            pltpu.VMEM((FOX_GROUP, tq, 1), F32),
            pltpu.VMEM((FOX_GROUP, tq, 2 * FOX_HD), F32),
        ],
    )
    return pl.pallas_call(
        functools.partial(_fox_kernel, tq=tq, tk=tk, nq=nq),
        grid_spec=grid_spec,
        out_shape=jax.ShapeDtypeStruct((b, s, FOX_W), BF16),
        compiler_params=_params(("arbitrary", "arbitrary")),
        name="fox",
    )(js, fq, fk, fv, cum, mask)


def _merge_kernel(x_ref, g_ref, wg_ref, bg_ref, yg_ref, yf_ref, mq_ref, mk_ref, mv_ref,
                  wb_ref, wo_ref, o_ref, *, tm, sub):
    d = x_ref.shape[-1]

    def sub_tile(r0, n):
        rows = slice(r0, r0 + n)
        x = x_ref[0, rows, :]
        h = (_rms(x) * g_ref[0]).astype(BF16)

        def gate_logits(i):
            return _dot(h, wg_ref[0, :, i * d:(i + 1) * d]) + bg_ref[0, :, i * d:(i + 1) * d]

        def head(hd):
            return slice(hd * MEM_HD, (hd + 1) * MEM_HD)

        yield
        scores = [_dot_nt(mq_ref[0, rows, head(hd)], mk_ref[0, 0, :, head(hd)])
                  for hd in range(MEM_HEADS)]
        g0, t0 = gate_logits(0), _dot(yg_ref[0, rows, :], wb_ref[0, 0])
        yield
        probs = []
        for s in scores:
            p = jnp.exp(s - jnp.max(s, axis=-1, keepdims=True))
            probs.append((p / jnp.sum(p, axis=-1, keepdims=True)).astype(BF16))
        g1, t1 = gate_logits(1), _dot(yf_ref[0, rows, :], wb_ref[0, 1])
        yield
        merged = _sigmoid(g0) * t0
        y_mem = jnp.concatenate([_dot(p, mv_ref[0, 0, :, head(hd)]) for hd, p in enumerate(probs)],
                                axis=-1).astype(BF16)
        g2 = gate_logits(2)
        yield
        merged = merged + _sigmoid(g1) * t1
        t2 = _dot(y_mem, wb_ref[0, 2])
        yield
        merged = merged + _sigmoid(g2) * t2
        o_ref[0, rows, :] = x + _dot(merged.astype(BF16), wo_ref[0])

    _run_staggered([sub_tile(r0, sub) for r0 in range(0, tm, sub)], lead=2)


def _merge(x, g_mix, w_gate, b_gate, y_gla, y_fox, mq, mk, mv, w_branch, w_out, *, layer, tm, sub):
    b, s, d = x.shape
    m = mk.shape[2]
    tok = lambda i, j: (i, j, 0)
    return pl.pallas_call(
        functools.partial(_merge_kernel, tm=tm, sub=sub),
        grid=(b, s // tm),
        in_specs=[
            pl.BlockSpec((1, tm, d), tok),
            _layer_spec(g_mix, layer),
            _layer_spec(w_gate, layer),
            _layer_spec(b_gate, layer),
            pl.BlockSpec((1, tm, GLA_V), tok),
            pl.BlockSpec((1, tm, FOX_W), tok),
            pl.BlockSpec((1, tm, MEM_W), tok),
            pl.BlockSpec((1, 1, m, MEM_W), lambda i, j: (layer, i, 0, 0)),
            pl.BlockSpec((1, 1, m, MEM_W), lambda i, j: (layer, i, 0, 0)),
            _layer_spec(w_branch, layer),
            _layer_spec(w_out, layer),
        ],
        out_specs=pl.BlockSpec((1, tm, d), tok),
        out_shape=jax.ShapeDtypeStruct((b, s, d), F32),
        compiler_params=_params(("arbitrary", "arbitrary")),
        name="merge",
    )(x, g_mix, w_gate, b_gate, y_gla, y_fox, mq, mk, mv, w_branch, w_out)


def _ffn_kernel(x_ref, g_ref, wg_ref, wu_ref, wd_ref, o_ref, *, tm, sub):
    def sub_tile(r0, n):
        rows = slice(r0, r0 + n)
        x = x_ref[0, rows, :]
        h = (_rms(x) * g_ref[0]).astype(BF16)
        yield
        gate = _dot(h, wg_ref[0])
        up = _dot(h, wu_ref[0])
        yield
        act = (gate * _sigmoid(gate) * up).astype(BF16)
        yield
        o_ref[0, rows, :] = x + _dot(act, wd_ref[0])

    _run_staggered([sub_tile(r0, sub) for r0 in range(0, tm, sub)], lead=2)


def _ffn(x, g_ffn, w_gate, w_up, w_down, *, layer, tm, sub):
    b, s, d = x.shape
    tok = lambda i, j: (i, j, 0)
    return pl.pallas_call(
        functools.partial(_ffn_kernel, tm=tm, sub=sub),
        grid=(b, s // tm),
        in_specs=[
            pl.BlockSpec((1, tm, d), tok),
            _layer_spec(g_ffn, layer),
            _layer_spec(w_gate, layer),
            _layer_spec(w_up, layer),
            _layer_spec(w_down, layer),
        ],
        out_specs=pl.BlockSpec((1, tm, d), tok),
        out_shape=jax.ShapeDtypeStruct((b, s, d), F32),
        compiler_params=_params(("arbitrary", "arbitrary")),
        name="ffn",
    )(x, g_ffn, w_gate, w_up, w_down)


def kernel(x, mem, g_mix, w_in, w_gla_a2, b_gla_a, g_gla_out, b_fox_f, g_fox_q, g_fox_k, g_mem, w_mem_kv, g_mem_q, g_mem_k, b_gate, w_branch, w_out, g_ffn, w_ffn_gate, w_ffn_up, w_ffn_down):
    depth, d = g_mix.shape
    o_ga = 2 * GLA_QK + 2 * GLA_V
    o_fox = o_ga + GLA_RANK
    o_ff = o_fox + 3 * FOX_W
    o_mq = o_ff + FOX_HEADS
    o_bg = o_mq + MEM_W

    mk, mv = _memkv(mem, _rows(g_mem), w_mem_kv, _rows(g_mem_k))
    w_a, w_g = _wprep(w_in, (o_ga, o_fox, o_ff, o_mq, o_bg), cb=WTRANS_COLS, rb=WPREP_ROWS)
    w2p = jnp.zeros((depth, SMALL_W, GLA_QK), F32).at[:, SMALL_GA:SMALL_GA + GLA_RANK].set(w_gla_a2).astype(BF16)
    b_small = jnp.zeros((depth, 1, SMALL_W), F32).at[:, 0, SMALL_FF:SMALL_FF + FOX_HEADS].set(b_fox_f)
    w_br, w_o = w_branch.astype(BF16), w_out.astype(BF16)
    w_fg, w_fu, w_fd = w_ffn_gate.astype(BF16), w_ffn_up.astype(BF16), w_ffn_down.astype(BF16)
    g_mix3, b_a3, g_go3, g_fq3, g_fk3, g_mq3, b_g3, g_ffn3 = map(
        _rows, (g_mix, b_gla_a, g_gla_out, g_fox_q, g_fox_k, g_mem_q, b_gate, g_ffn))

    for l in range(depth):
        y_gla, fq, fk, fv, mq, cum = _proj(x, g_mix3, w_a, w2p, b_a3, b_small, g_fq3, g_fk3, g_mq3, g_go3,
                                           layer=l, tm=TOKEN_TILE, sub=PROJ_SUB)
        y_fox = _fox(fq, fk, fv, cum, g_fox_q[l], g_fox_k[l], tq=FOX_TQ, tk=FOX_TK)
        x = _merge(x, g_mix3, w_g, b_g3, y_gla, y_fox, mq, mk, mv, w_br, w_o,
                   layer=l, tm=TOKEN_TILE, sub=MERGE_SUB)
        x = _ffn(x, g_ffn3, w_fg, w_fu, w_fd, layer=l, tm=TOKEN_TILE, sub=FFN_SUB)
    return x
```
